```python
import math
import jax, jax.numpy as jnp
from jax import lax
import numpy as np

D_MODEL = 1024
BATCH = 1
SEQ = 16384
DEPTH = 2

N_A_LAYERS = DEPTH // 2
N_B_LAYERS = DEPTH - N_A_LAYERS
PLE_DIM = 256
DN_ALPHA = (2.0 * DEPTH) ** 0.25
DN_BETA = (8.0 * DEPTH) ** -0.25
NORM_EPS = 1e-5

ML_HEADS = 8
ML_QK_DIM = D_MODEL // (2 * ML_HEADS)
ML_V_DIM = D_MODEL // ML_HEADS
ML_CHUNK = 64
ML_Q_END = ML_HEADS * ML_QK_DIM
ML_K_END = 2 * ML_HEADS * ML_QK_DIM
ML_V_END = ML_K_END + ML_HEADS * ML_V_DIM
ML_I_END = ML_V_END + ML_HEADS
ML_F_END = ML_I_END + ML_HEADS
ML_IN_COLS = ML_F_END + D_MODEL

MLA_HEADS = 8
MLA_NOPE_DIM = 128
MLA_ROPE_DIM = 64
MLA_V_DIM = 128
MLA_KV_RANK = 256
MLA_Q_RANK = 384
ROPE_THETA = 10000.0
ATTN_BLOCK = 128

PEER_HEADS = 8
PEER_N_KEYS = 128
PEER_N_EXPERTS = PEER_N_KEYS * PEER_N_KEYS
PEER_KEY_DIM = 256
PEER_HALF = PEER_KEY_DIM // 2
PEER_TOPK = 16
PEER_BLOCK = 128

kernel_name = "yoco_mlstm_mla_peer_deepnorm"


def layer_norm(x, g, b):
    xf = x.astype(jnp.float32)
    mu = jnp.mean(xf, -1, keepdims=True)
    var = jnp.mean(jnp.square(xf - mu), -1, keepdims=True)
    y = (xf - mu) * lax.rsqrt(var + NORM_EPS)
    return (y * g.astype(jnp.float32) + b.astype(jnp.float32)).astype(x.dtype)


def rms_norm(x, g):
    xf = x.astype(jnp.float32)
    y = xf * lax.rsqrt(jnp.mean(jnp.square(xf), -1, keepdims=True) + NORM_EPS)
    return (y * g.astype(jnp.float32)).astype(x.dtype)


def rope_angles(positions, dim):
    inv_freq = ROPE_THETA ** (-jnp.arange(0, dim, 2, dtype=jnp.float32) / dim)
    ang = positions.astype(jnp.float32)[..., None] * inv_freq
    return jnp.cos(ang), jnp.sin(ang)


def apply_rope(x, cos, sin):
    xf = x.astype(jnp.float32)
    x1, x2 = jnp.split(xf, 2, axis=-1)
    return jnp.concatenate([x1 * cos - x2 * sin, x1 * sin + x2 * cos], -1).astype(x.dtype)


def mlstm_chunk_step(carry, inp):
    C, n, m = carry
    q, k, v, ig, lf = inp
    L = q.shape[2]
    b = jnp.cumsum(lf, axis=-1)
    causal = jnp.tril(jnp.ones((L, L), dtype=bool))
    log_w = jnp.where(causal, b[..., :, None] - b[..., None, :] + ig[..., None, :], -jnp.inf)
    log_inter = b + m[..., None]
    m_t = jnp.maximum(log_inter, jnp.max(log_w, -1))
    w = jnp.exp(log_w - m_t[..., None])
    s_inter = jnp.exp(log_inter - m_t)
    qk = jnp.einsum('bhtd,bhsd->bhts', q, k) * w
    num = jnp.einsum('bhts,bhsv->bhtv', qk, v) + s_inter[..., None] * jnp.einsum('bhtd,bhdv->bhtv', q, C)
    den = jnp.sum(qk, -1) + s_inter * jnp.einsum('bhtd,bhd->bht', q, n)
    h = num / jnp.maximum(jnp.abs(den), jnp.exp(-m_t))[..., None]
    b_last = b[..., -1]
    log_src = b_last[..., None] - b + ig
    m_new = jnp.maximum(b_last + m, jnp.max(log_src, -1))
    w_src = jnp.exp(log_src - m_new[..., None])
    s_old = jnp.exp(b_last + m - m_new)
    C_new = s_old[..., None, None] * C + jnp.einsum('bhs,bhsd,bhsv->bhdv', w_src, k, v)
    n_new = s_old[..., None] * n + jnp.einsum('bhs,bhsd->bhd', w_src, k)
    return (C_new, n_new, m_new), h


def mlstm_mixer(x, w_in, b_if, hn_g, w_out):
    B, S, _ = x.shape
    H, L = ML_HEADS, ML_CHUNK
    nc = S // L
    proj = x @ w_in
    q, k, v = proj[..., :ML_Q_END], proj[..., ML_Q_END:ML_K_END], proj[..., ML_K_END:ML_V_END]
    gi, gf, og = proj[..., ML_V_END:ML_I_END], proj[..., ML_I_END:ML_F_END], proj[..., ML_F_END:]

    def to_chunks(t, d):
        return t.reshape(B, nc, L, H, d).transpose(1, 0, 3, 2, 4).astype(jnp.float32)

    def gate_chunks(t):
        return t.reshape(B, nc, L, H).transpose(1, 0, 3, 2)

    qc = to_chunks(q, ML_QK_DIM)
    kc = to_chunks(k, ML_QK_DIM) * (ML_QK_DIM ** -0.5)
    vc = to_chunks(v, ML_V_DIM)
    b_if = b_if.astype(jnp.float32)
    ig = gate_chunks(gi.astype(jnp.float32) + b_if[0])
    lf = gate_chunks(jax.nn.log_sigmoid(gf.astype(jnp.float32) + b_if[1]))
    init = (jnp.zeros((B, H, ML_QK_DIM, ML_V_DIM), jnp.float32),
            jnp.zeros((B, H, ML_QK_DIM), jnp.float32),
            jnp.zeros((B, H), jnp.float32))
    _, h = lax.scan(mlstm_chunk_step, init, (qc, kc, vc, ig, lf))
    h = h.transpose(1, 0, 3, 2, 4).reshape(B, S, H, ML_V_DIM)
    mu = jnp.mean(h, -1, keepdims=True)
    var = jnp.mean(jnp.square(h - mu), -1, keepdims=True)
    hn = (h - mu) * lax.rsqrt(var + NORM_EPS) * hn_g.astype(jnp.float32).reshape(H, ML_V_DIM)
    out = jax.nn.sigmoid(og.astype(jnp.float32)) * hn.reshape(B, S, H * ML_V_DIM)
    return out.astype(x.dtype) @ w_out


def mla_shared_kv(xs, w_down, kv_norm_g, w_up, cos, sin):
    B, S, _ = xs.shape
    ckr = xs @ w_down
    c_kv = rms_norm(ckr[..., :MLA_KV_RANK], kv_norm_g)
    kv = (c_kv @ w_up).reshape(B, S, MLA_HEADS, MLA_NOPE_DIM + MLA_V_DIM)
    k_nope = kv[..., :MLA_NOPE_DIM].transpose(0, 2, 1, 3)
    v = kv[..., MLA_NOPE_DIM:].transpose(0, 2, 1, 3)
    k_rope = apply_rope(ckr[..., MLA_KV_RANK:], cos, sin)
    return k_nope, k_rope, v


def mla_mixer(x, w_dq, q_norm_g, w_uq, w_out, k_nope, k_rope, v, cos, sin):
    B, S, _ = x.shape
    H = MLA_HEADS
    c_q = rms_norm(x @ w_dq, q_norm_g)
    q = (c_q @ w_uq).reshape(B, S, H, MLA_NOPE_DIM + MLA_ROPE_DIM).transpose(0, 2, 1, 3)
    q_nope = q[..., :MLA_NOPE_DIM]
    q_rope = apply_rope(q[..., MLA_NOPE_DIM:], cos[:, None], sin[:, None])
    nb = S // ATTN_BLOCK

    def blocks(t):
        return t.reshape(B, H, nb, ATTN_BLOCK, t.shape[-1]).transpose(2, 0, 1, 3, 4)

    scale = (MLA_NOPE_DIM + MLA_ROPE_DIM) ** -0.5
    key_pos = jnp.arange(S)

    def attend(args):
        idx, qn, qr = args
        s = jnp.einsum('bhqd,bhkd->bhqk', qn, k_nope) + jnp.einsum('bhqd,bkd->bhqk', qr, k_rope)
        s = s.astype(jnp.float32) * scale
        q_pos = idx * ATTN_BLOCK + jnp.arange(ATTN_BLOCK)
        s = jnp.where(key_pos[None, :] <= q_pos[:, None], s, -jnp.inf)
        p = jax.nn.softmax(s, axis=-1).astype(v.dtype)
        return jnp.einsum('bhqk,bhkd->bhqd', p, v)

    o = lax.map(attend, (jnp.arange(nb), blocks(q_nope), blocks(q_rope)))
    o = o.transpose(1, 0, 3, 2, 4).reshape(B, S, H * MLA_V_DIM)
    return o @ w_out


def peer_ffn(x, w_q, sub_keys, u, v):
    B, S, D = x.shape
    T = B * S
    K = PEER_TOPK
    xt = x.reshape(T, D)
    q = (xt @ w_q).reshape(T, PEER_HEADS, 2, PEER_HALF)
    s = jnp.einsum('thcd,hcnd->thcn', q, sub_keys).astype(jnp.float32)
    s1, i1 = lax.top_k(s[:, :, 0], K)
    s2, i2 = lax.top_k(s[:, :, 1], K)
    cand_s = (s1[..., :, None] + s2[..., None, :]).reshape(T, PEER_HEADS, K * K)
    cand_i = (i1[..., :, None] * PEER_N_KEYS + i2[..., None, :]).reshape(T, PEER_HEADS, K * K)
    top_s, top_pos = lax.top_k(cand_s, K)
    expert_idx = jnp.take_along_axis(cand_i, top_pos, axis=-1)
    gates = jax.nn.softmax(top_s, axis=-1).astype(x.dtype)
    nb = T // PEER_BLOCK

    def expert_block(args):
        xb, eb, gb = args
        ub = jnp.take(u, eb, axis=0)
        act = jax.nn.gelu(jnp.einsum('thkd,td->thk', ub, xb), approximate=False)
        vb = jnp.take(v, eb, axis=0)
        return jnp.einsum('thk,thkd->td', gb * act, vb)

    out = lax.map(expert_block, (xt.reshape(nb, PEER_BLOCK, D),
                                 expert_idx.reshape(nb, PEER_BLOCK, PEER_HEADS, K),
                                 gates.reshape(nb, PEER_BLOCK, PEER_HEADS, K)))
    return out.reshape(B, S, D)


def setup_inputs(seed: int = 0) -> dict:
    key = jax.random.key(seed)
    ks = jax.random.split(key, 24)
    D = D_MODEL
    nrm = jax.random.normal
    f32 = jnp.float32
    x = nrm(ks[0], (BATCH, SEQ, D), f32)
    p = nrm(ks[1], (DEPTH, BATCH, SEQ, PLE_DIM), f32)
    positions = jnp.broadcast_to(jnp.arange(SEQ, dtype=jnp.int32), (BATCH, SEQ))
    ln_g = 1.0 + 0.05 * nrm(ks[2], (DEPTH, 2, D), f32)
    ln_b = 0.02 * nrm(ks[3], (DEPTH, 2, D), f32)
    a_w_in = nrm(ks[4], (N_A_LAYERS, D, ML_IN_COLS), f32) * D ** -0.5
    b_i = 0.1 * nrm(ks[5], (N_A_LAYERS, ML_HEADS), f32)
    b_f = jnp.linspace(3.0, 6.0, ML_HEADS, dtype=f32) + 0.1 * nrm(ks[6], (N_A_LAYERS, ML_HEADS), f32)
    a_b_if = jnp.stack([b_i, b_f], axis=1)
    a_hn_g = 1.0 + 0.05 * nrm(ks[7], (N_A_LAYERS, D), f32)
    a_w_out = nrm(ks[8], (N_A_LAYERS, D, D), f32) * (D ** -0.5 * DN_BETA)
    kv_w_down = nrm(ks[9], (D, MLA_KV_RANK + MLA_ROPE_DIM), f32) * D ** -0.5
    kv_norm_g = 1.0 + 0.05 * nrm(ks[10], (MLA_KV_RANK,), f32)
    kv_w_up = nrm(ks[11], (MLA_KV_RANK, MLA_HEADS * (MLA_NOPE_DIM + MLA_V_DIM)), f32) * MLA_KV_RANK ** -0.5
    b_w_dq = nrm(ks[12], (N_B_LAYERS, D, MLA_Q_RANK), f32) * D ** -0.5
    b_q_norm_g = 1.0 + 0.05 * nrm(ks[13], (N_B_LAYERS, MLA_Q_RANK), f32)
    b_w_uq = nrm(ks[14], (N_B_LAYERS, MLA_Q_RANK, MLA_HEADS * (MLA_NOPE_DIM + MLA_ROPE_DIM)), f32) * MLA_Q_RANK ** -0.5
    b_w_out = nrm(ks[15], (N_B_LAYERS, MLA_HEADS * MLA_V_DIM, D), f32) * ((MLA_HEADS * MLA_V_DIM) ** -0.5 * DN_BETA)
    peer_w_q = nrm(ks[16], (DEPTH, D, PEER_HEADS * PEER_KEY_DIM), f32) * D ** -0.5
    peer_sub_keys = nrm(ks[17], (DEPTH, PEER_HEADS, 2, PEER_N_KEYS, PEER_HALF), f32) * PEER_HALF ** -0.5
    peer_u = nrm(ks[18], (DEPTH, PEER_N_EXPERTS, D), f32) * D ** -0.5
    peer_v = nrm(ks[19], (DEPTH, PEER_N_EXPERTS, D), f32) * (DN_BETA * PEER_HEADS ** -0.5)
    ple_w_proj = nrm(ks[20], (DEPTH, PLE_DIM, D), f32) * (0.5 * PLE_DIM ** -0.5)
    ple_w_gate = nrm(ks[21], (DEPTH, D, D), f32) * D ** -0.5
    return {"x": x, "p": p, "positions": positions, "ln_g": ln_g, "ln_b": ln_b,
            "a_w_in": a_w_in, "a_b_if": a_b_if, "a_hn_g": a_hn_g, "a_w_out": a_w_out,
            "kv_w_down": kv_w_down, "kv_norm_g": kv_norm_g, "kv_w_up": kv_w_up,
            "b_w_dq": b_w_dq, "b_q_norm_g": b_q_norm_g, "b_w_uq": b_w_uq, "b_w_out": b_w_out,
            "peer_w_q": peer_w_q, "peer_sub_keys": peer_sub_keys, "peer_u": peer_u, "peer_v": peer_v,
            "ple_w_proj": ple_w_proj, "ple_w_gate": ple_w_gate}


def reference(x, p, positions, ln_g, ln_b, a_w_in, a_b_if, a_hn_g, a_w_out,
              kv_w_down, kv_norm_g, kv_w_up, b_w_dq, b_q_norm_g, b_w_uq, b_w_out,
              peer_w_q, peer_sub_keys, peer_u, peer_v, ple_w_proj, ple_w_gate):
    cos, sin = rope_angles(positions, MLA_ROPE_DIM)
    shared = None
    for i in range(DEPTH):
        if i < N_A_LAYERS:
            mix = mlstm_mixer(x, a_w_in[i], a_b_if[i], a_hn_g[i], a_w_out[i])
        else:
            j = i - N_A_LAYERS
            if j == 0:
                shared = mla_shared_kv(x, kv_w_down, kv_norm_g, kv_w_up, cos, sin)
            k_nope, k_rope, v = shared
            mix = mla_mixer(x, b_w_dq[j], b_q_norm_g[j], b_w_uq[j], b_w_out[j],
                            k_nope, k_rope, v, cos, sin)
        x = layer_norm(DN_ALPHA * x + mix, ln_g[i, 0], ln_b[i, 0])
        x = layer_norm(DN_ALPHA * x + peer_ffn(x, peer_w_q[i], peer_sub_keys[i], peer_u[i], peer_v[i]),
                       ln_g[i, 1], ln_b[i, 1])
        x = x + jax.nn.sigmoid(x @ ple_w_gate[i]) * (p[i] @ ple_w_proj[i])
    return x
```

```python
import functools
import math

import jax
import jax.numpy as jnp
from jax import lax
from jax.experimental import pallas as pl
from jax.experimental.pallas import tpu as pltpu

F32 = jnp.float32
BF16 = jnp.bfloat16

LANES = 128
D_MODEL = 1024
DEPTH = 2
DN_ALPHA = (2.0 * DEPTH) ** 0.25
NORM_EPS = 1e-5

ML_HEADS = 8
ML_QK = 64
ML_V = 128
ML_CHUNK = 256

MLA_HEADS = 8
MLA_NOPE = 128
MLA_ROPE = 64
MLA_V = 128
MLA_KV_RANK = 256
MLA_Q_RANK = 384
MLA_QK_PAD = 256
ROPE_THETA = 10000.0
ATTN_BLOCK = 512

PEER_HEADS = 8
PEER_KEYS = 128
PEER_HALF = 128
PEER_TOPK = 16
PLE_DIM = 256

ROW_TILE = 512
ROUTE_TILE = 256
EXPERT_TOK_TILE = 512
EXPERT_TILE = 1024
VMEM_LIMIT = 56 * 1024 * 1024

NEG_INF = float("-inf")


def _params(*sem):
    return pltpu.CompilerParams(dimension_semantics=sem, vmem_limit_bytes=VMEM_LIMIT)


def _dot(a, b):
    return jnp.dot(a, b, preferred_element_type=F32)


def _dot_nt(a, b):
    return lax.dot_general(a, b, (((1,), (1,)), ((), ())), preferred_element_type=F32)


def _layer_norm(y, g, b):
    mu = jnp.mean(y, axis=-1, keepdims=True)
    yc = y - mu
    var = jnp.mean(yc * yc, axis=-1, keepdims=True)
    return yc * lax.rsqrt(var + NORM_EPS) * g + b


def _rms_norm(y, g):
    return y * lax.rsqrt(jnp.mean(y * y, axis=-1, keepdims=True) + NORM_EPS) * g


def _full(shape):
    nd = len(shape)
    return pl.BlockSpec(shape, lambda *_: (0,) * nd)


def _mlstm_proj_kernel(x_ref, wqv_ref, wog_ref, wg_ref, wkt_ref, wgt_ref,
                       qv_ref, og_ref, g_ref, kt_ref, gt_ref):
    xb = x_ref[...].astype(BF16)
    qv_ref[...] = _dot(xb, wqv_ref[...]).astype(BF16)
    og_ref[...] = _dot(xb, wog_ref[...])
    g_ref[...] = _dot(xb, wg_ref[...])
    kt_ref[...] = _dot_nt(wkt_ref[...], xb).astype(BF16)
    gt_ref[...] = _dot_nt(wgt_ref[...], xb)


def _mlstm_proj(x, wqv, wog, wg, wkt, wgt):
    T, D = x.shape
    tm = ROW_TILE
    nqv, nk, ng = wqv.shape[1], wkt.shape[0], wgt.shape[0]
    return pl.pallas_call(
        _mlstm_proj_kernel,
        grid=(T // tm,),
        in_specs=[pl.BlockSpec((tm, D), lambda i: (i, 0)),
                  _full(wqv.shape), _full(wog.shape), _full(wg.shape), _full(wkt.shape), _full(wgt.shape)],
        out_specs=[pl.BlockSpec((tm, nqv), lambda i: (i, 0)),
                   pl.BlockSpec((tm, D), lambda i: (i, 0)),
                   pl.BlockSpec((tm, LANES), lambda i: (i, 0)),
                   pl.BlockSpec((nk, tm), lambda i: (0, i)),
                   pl.BlockSpec((ng, tm), lambda i: (0, i))],
        out_shape=[jax.ShapeDtypeStruct((T, nqv), BF16),
                   jax.ShapeDtypeStruct((T, D), F32),
                   jax.ShapeDtypeStruct((T, LANES), F32),
                   jax.ShapeDtypeStruct((nk, T), BF16),
                   jax.ShapeDtypeStruct((ng, T), F32)],
        compiler_params=_params("parallel"),
    )(x, wqv, wog, wg, wkt, wgt)


def _split3(x):
    hi = x.astype(BF16)
    r1 = x - hi.astype(F32)
    mid = r1.astype(BF16)
    lo = (r1 - mid.astype(F32)).astype(BF16)
    return hi, mid, lo


def _log_sigmoid(x):
    return jnp.minimum(x, 0.0) - jnp.log1p(jnp.exp(-jnp.abs(x)))


def _mlstm_scan_kernel(q_ref, v_ref, kt_ref, g_ref, gt_ref, og_ref, bc_ref, br_ref, hng_ref,
                       o_ref, c_ref, m_ref):
    H = ML_HEADS
    L = q_ref.shape[0]

    @pl.when(pl.program_id(0) == 0)
    def _():
        c_ref[...] = jnp.zeros_like(c_ref)
        m_ref[...] = jnp.zeros_like(m_ref)

    row = lax.broadcasted_iota(jnp.int32, (L, L), 0)
    col = lax.broadcasted_iota(jnp.int32, (L, L), 1)
    causal = col <= row
    tril = jnp.where(causal, 1.0, 0.0).astype(BF16)
    triu = jnp.where(row <= col, 1.0, 0.0).astype(BF16)

    g = g_ref[...] + bc_ref[...]
    gt = gt_ref[...] + br_ref[...]
    b_cols = sum(_dot(tril, piece) for piece in _split3(_log_sigmoid(g)))
    b_rows = sum(_dot(piece, triu) for piece in _split3(_log_sigmoid(gt)))

    lane = lax.broadcasted_iota(jnp.int32, (L, LANES), 1)
    ones_col = jnp.where(lane == 0, 1.0, 0.0).astype(BF16)

    for h in range(H):
        b_c = b_cols[:, H + h:H + h + 1]
        b_r = b_rows[H + h:H + h + 1, :]
        ig_r = gt[h:h + 1, :]
        m_prev = m_ref[h:h + 1, 0:1]
        dmat = jnp.where(causal, b_c - b_r + ig_r, NEG_INF)
        log_inter = b_c + m_prev
        m_t = jnp.maximum(log_inter, jnp.max(dmat, axis=1, keepdims=True))
        w = jnp.exp(dmat - m_t)
        s_inter = jnp.exp(log_inter - m_t)

        q = q_ref[:, h * LANES:(h + 1) * LANES]
        kt = kt_ref[h * LANES:(h + 1) * LANES, :]
        v_ext = jnp.concatenate([v_ref[:, h * ML_V:(h + 1) * ML_V], ones_col], axis=1)
        s = (_dot(q, kt) * w).astype(BF16)
        c_prev = c_ref[h]
        nd = _dot(s, v_ext) + s_inter * _dot(q, c_prev.astype(BF16))
        num = nd[:, :ML_V]
        den = nd[:, ML_V:ML_V + 1]
        hh = num / jnp.maximum(jnp.abs(den), jnp.exp(-m_t))

        b_last = b_r[:, L - 1:L]
        log_src = b_last - b_r + ig_r
        m_new = jnp.maximum(b_last + m_prev, jnp.max(log_src, axis=1, keepdims=True))
        w_src = jnp.exp(log_src - m_new)
        s_old = jnp.exp(b_last + m_prev - m_new)
        c_ref[h] = s_old * c_prev + _dot((kt.astype(F32) * w_src).astype(BF16), v_ext)
        m_ref[h:h + 1, :] = jnp.broadcast_to(m_new, (1, LANES))

        mu = jnp.mean(hh, axis=-1, keepdims=True)
        hc = hh - mu
        var = jnp.mean(hc * hc, axis=-1, keepdims=True)
        hn = hc * lax.rsqrt(var + NORM_EPS) * hng_ref[:, h * ML_V:(h + 1) * ML_V]
        og = og_ref[:, h * ML_V:(h + 1) * ML_V]
        o_ref[:, h * ML_V:(h + 1) * ML_V] = (jax.nn.sigmoid(og) * hn).astype(o_ref.dtype)


def _mlstm_scan(qv, kt, g, gt, og, bias_cols, bias_rows, hn_g):
    T = qv.shape[0]
    L = ML_CHUNK
    D = D_MODEL
    return pl.pallas_call(
        _mlstm_scan_kernel,
        grid=(T // L,),
        in_specs=[pl.BlockSpec((L, D), lambda c: (c, 0)),
                  pl.BlockSpec((L, D), lambda c: (c, 1)),
                  pl.BlockSpec((D, L), lambda c: (0, c)),
                  pl.BlockSpec((L, LANES), lambda c: (c, 0)),
                  pl.BlockSpec((2 * ML_HEADS, L), lambda c: (0, c)),
                  pl.BlockSpec((L, D), lambda c: (c, 0)),
                  _full(bias_cols.shape), _full(bias_rows.shape), _full(hn_g.shape)],
        out_specs=pl.BlockSpec((L, D), lambda c: (c, 0)),
        out_shape=jax.ShapeDtypeStruct((T, D), BF16),
        scratch_shapes=[pltpu.VMEM((ML_HEADS, LANES, 2 * LANES), F32),
                        pltpu.VMEM((ML_HEADS, LANES), F32)],
        compiler_params=_params("arbitrary"),
    )(qv, qv, kt, g, gt, og, bias_cols, bias_rows, hn_g)


def _mix_ln_kernel(a_ref, w_ref, x_ref, g_ref, b_ref, o_ref):
    y = _dot(a_ref[...], w_ref[...]) + DN_ALPHA * x_ref[...]
    o_ref[...] = _layer_norm(y, g_ref[...], b_ref[...])


def _mix_ln(a, w, x, g, b):
    T, D = x.shape
    tm = ROW_TILE
    return pl.pallas_call(
        _mix_ln_kernel,
        grid=(T // tm,),
        in_specs=[pl.BlockSpec((tm, a.shape[1]), lambda i: (i, 0)), _full(w.shape),
                  pl.BlockSpec((tm, D), lambda i: (i, 0)), _full(g.shape), _full(b.shape)],
        out_specs=pl.BlockSpec((tm, D), lambda i: (i, 0)),
        out_shape=jax.ShapeDtypeStruct((T, D), F32),
        compiler_params=_params("parallel"),
    )(a, w, x, g, b)


PEER_NTOP = PEER_TOPK + 1
PEER_TOP_ROWS = 24


def _peer_candidate_blocks(a_ref, b_ref):
    blocks = [a_ref[...] + b_ref[0:1, :], b_ref[...] + a_ref[0:1, :]]
    for j in range(1, 8):
        blocks.append(a_ref[0:8, :] + b_ref[j:j + 1, :])
    return jnp.concatenate(blocks, axis=0)


def _peer_route_kernel(x_ref, wq_ref, keys_ref, dm_ref, w2_ref, c1_ref, g1_ref, top_ref):
    xb = x_ref[...].astype(BF16)
    q = _dot(xb, wq_ref[...]).astype(BF16)
    top_ref[...] = jnp.full(top_ref.shape, NEG_INF, F32)
    scores = []
    for hc in range(2 * PEER_HEADS):
        s = _dot_nt(keys_ref[hc], q[:, hc * PEER_HALF:(hc + 1) * PEER_HALF])
        scores.append(s)
        cur = s
        for r in range(PEER_NTOP):
            mx = jnp.max(cur, axis=0, keepdims=True)
            top_ref[hc, r:r + 1, :] = mx
            cur = jnp.where(cur == mx, NEG_INF, cur)
    for h in range(PEER_HEADS):
        s1, s2 = scores[2 * h], scores[2 * h + 1]
        cand = _peer_candidate_blocks(top_ref.at[2 * h], top_ref.at[2 * h + 1])
        a0 = top_ref[2 * h, 0:1, :]
        b0 = top_ref[2 * h + 1, 0:1, :]
        best = a0 + b0
        z = jnp.zeros_like(best)
        t_prev = best
        t_cur = best
        for r in range(PEER_NTOP):
            t_prev = t_cur
            t_cur = jnp.max(cand, axis=0, keepdims=True)
            if r < PEER_TOPK:
                z = z + jnp.exp(t_cur - best)
            cand = jnp.where(cand == t_cur, NEG_INF, cand)
        tau = 0.5 * (t_prev + t_cur)
        dm_ref[h] = s2 - tau
        w2_ref[h] = jnp.exp(s2 - b0)
        c1_ref[h] = -s1
        g1_ref[h] = jnp.exp(s1 - a0) / z


def _peer_route(x, wq, keys):
    T, D = x.shape
    tt = ROUTE_TILE
    out = jax.ShapeDtypeStruct((PEER_HEADS, PEER_KEYS, T), F32)
    ospec = pl.BlockSpec((PEER_HEADS, PEER_KEYS, tt), lambda i: (0, 0, i))
    return pl.pallas_call(
        _peer_route_kernel,
        grid=(T // tt,),
        in_specs=[pl.BlockSpec((tt, D), lambda i: (i, 0)), _full(wq.shape), _full(keys.shape)],
        out_specs=[ospec] * 4,
        out_shape=[out] * 4,
        scratch_shapes=[pltpu.VMEM((2 * PEER_HEADS, PEER_TOP_ROWS, tt), F32)],
        compiler_params=_params("parallel"),
    )(x, wq, keys)


def _gelu(x):
    return 0.5 * x * (1.0 + lax.erf(x * (1.0 / math.sqrt(2.0))))


def _peer_expert_kernel(x_ref, u_ref, vt_ref, dm_ref, w2_ref, c1_ref, g1_ref, o_ref, acc_ref, p_ref):
    j = pl.program_id(1)
    tt = x_ref.shape[0]
    n_sub = u_ref.shape[0] // PEER_KEYS

    @pl.when(j == 0)
    def _():
        acc_ref[...] = jnp.zeros_like(acc_ref)

    act = _dot_nt(u_ref[...], x_ref[...])
    for a in range(n_sub):
        for lc in range(tt // LANES):
            ls = slice(lc * LANES, (lc + 1) * LANES)
            gate = jnp.zeros((PEER_KEYS, LANES), F32)
            for h in range(PEER_HEADS):
                c_row = c1_ref[h, a:a + 1, ls]
                g_row = g1_ref[h, a:a + 1, ls]
                gate = gate + jnp.where(dm_ref[h, :, ls] >= c_row, w2_ref[h, :, ls], 0.0) * g_row
            a_blk = act[a * PEER_KEYS:(a + 1) * PEER_KEYS, ls]
            p_ref[a * PEER_KEYS:(a + 1) * PEER_KEYS, ls] = (gate * _gelu(a_blk)).astype(BF16)
    acc_ref[...] += _dot(vt_ref[...], p_ref[...])

    @pl.when(j == pl.num_programs(1) - 1)
    def _():
        o_ref[...] = acc_ref[...].T


def _peer_expert(xb, u, vt, dm, w2, c1, g1):
    T, D = xb.shape
    E = u.shape[0]
    tt, et = EXPERT_TOK_TILE, EXPERT_TILE
    rspec = pl.BlockSpec((PEER_HEADS, PEER_KEYS, tt), lambda i, j: (0, 0, i))
    sspec = pl.BlockSpec((PEER_HEADS, et // PEER_KEYS, tt), lambda i, j: (0, j, i))
    return pl.pallas_call(
        _peer_expert_kernel,
        grid=(T // tt, E // et),
        in_specs=[pl.BlockSpec((tt, D), lambda i, j: (i, 0)),
                  pl.BlockSpec((et, D), lambda i, j: (j, 0)),
                  pl.BlockSpec((D, et), lambda i, j: (0, j)),
                  rspec, rspec, sspec, sspec],
        out_specs=pl.BlockSpec((tt, D), lambda i, j: (i, 0)),
        out_shape=jax.ShapeDtypeStruct((T, D), F32),
        scratch_shapes=[pltpu.VMEM((D, tt), F32), pltpu.VMEM((et, tt), BF16)],
        compiler_params=_params("parallel", "arbitrary"),
    )(xb, u, vt, dm, w2, c1, g1)


def _ffn_ln_ple_kernel(x_ref, f_ref, p_ref, g_ref, b_ref, wg_ref, wp_ref, o_ref, ob_ref):
    x2 = _layer_norm(DN_ALPHA * x_ref[...] + f_ref[...], g_ref[...], b_ref[...])
    gate = jax.nn.sigmoid(_dot(x2.astype(BF16), wg_ref[...]))
    pe = _dot(p_ref[...].astype(BF16), wp_ref[...])
    out = x2 + gate * pe
    o_ref[...] = out
    ob_ref[...] = out.astype(BF16)


def _ffn_ln_ple(x, f, p, g, b, wg, wp):
    T, D = x.shape
    tm = ROW_TILE
    return pl.pallas_call(
        _ffn_ln_ple_kernel,
        grid=(T // tm,),
        in_specs=[pl.BlockSpec((tm, D), lambda i: (i, 0)), pl.BlockSpec((tm, D), lambda i: (i, 0)),
                  pl.BlockSpec((tm, p.shape[1]), lambda i: (i, 0)),
                  _full(g.shape), _full(b.shape), _full(wg.shape), _full(wp.shape)],
        out_specs=[pl.BlockSpec((tm, D), lambda i: (i, 0))] * 2,
        out_shape=[jax.ShapeDtypeStruct((T, D), F32), jax.ShapeDtypeStruct((T, D), BF16)],
        compiler_params=_params("parallel"),
    )(x, f, p, g, b, wg, wp)


def _cast_kernel(x_ref, o_ref):
    o_ref[...] = x_ref[...].astype(o_ref.dtype)


def _to_bf16(x):
    T, D = x.shape
    tm = ROW_TILE
    return pl.pallas_call(
        _cast_kernel, grid=(T // tm,),
        in_specs=[pl.BlockSpec((tm, D), lambda i: (i, 0))],
        out_specs=pl.BlockSpec((tm, D), lambda i: (i, 0)),
        out_shape=jax.ShapeDtypeStruct((T, D), BF16),
        compiler_params=_params("parallel"),
    )(x)


def _mla_kv_kernel(x_ref, wd_ref, g_ref, wkn_ref, wv_ref, cos_ref, sin_ref, k_ref, v_ref):
    ckr = _dot(x_ref[...], wd_ref[...])
    ckv = _rms_norm(ckr[:, :MLA_KV_RANK], g_ref[...]).astype(BF16)
    kn = _dot(ckv, wkn_ref[...])
    v_ref[...] = _dot(ckv, wv_ref[...]).astype(BF16)
    kr = (ckr[:, MLA_KV_RANK:MLA_KV_RANK + LANES] * cos_ref[...]
          + ckr[:, MLA_KV_RANK + LANES:MLA_KV_RANK + 2 * LANES] * sin_ref[...]).astype(BF16)
    for h in range(MLA_HEADS):
        k_ref[:, h * MLA_QK_PAD:h * MLA_QK_PAD + MLA_NOPE] = kn[:, h * MLA_NOPE:(h + 1) * MLA_NOPE].astype(BF16)
        k_ref[:, h * MLA_QK_PAD + MLA_NOPE:(h + 1) * MLA_QK_PAD] = kr


def _mla_kv(xb, wd, g, wkn, wv, cos, sin):
    T, D = xb.shape
    tm = ROW_TILE
    return pl.pallas_call(
        _mla_kv_kernel,
        grid=(T // tm,),
        in_specs=[pl.BlockSpec((tm, D), lambda i: (i, 0)), _full(wd.shape), _full(g.shape),
                  _full(wkn.shape), _full(wv.shape),
                  pl.BlockSpec((tm, LANES), lambda i: (i, 0)), pl.BlockSpec((tm, LANES), lambda i: (i, 0))],
        out_specs=[pl.BlockSpec((tm, MLA_HEADS * MLA_QK_PAD), lambda i: (i, 0)),
                   pl.BlockSpec((tm, MLA_HEADS * MLA_V), lambda i: (i, 0))],
        out_shape=[jax.ShapeDtypeStruct((T, MLA_HEADS * MLA_QK_PAD), BF16),
                   jax.ShapeDtypeStruct((T, MLA_HEADS * MLA_V), BF16)],
        compiler_params=_params("parallel"),
    )(xb, wd, g, wkn, wv, cos, sin)


def _mla_q_kernel(x_ref, wdq_ref, g_ref, wuq_ref, cos_ref, sin_ref, q_ref):
    scale = (MLA_NOPE + MLA_ROPE) ** -0.5
    cq = _rms_norm(_dot(x_ref[...], wdq_ref[...]), g_ref[...]).astype(BF16)
    qa = _dot(cq, wuq_ref[...])
    rot0 = MLA_HEADS * MLA_QK_PAD
    for h in range(MLA_HEADS):
        base = h * MLA_QK_PAD
        q_ref[:, base:base + MLA_NOPE] = (qa[:, base:base + MLA_NOPE] * scale).astype(BF16)
        rope = (qa[:, base + MLA_NOPE:base + MLA_QK_PAD] * cos_ref[...]
                + qa[:, rot0 + h * LANES:rot0 + (h + 1) * LANES] * sin_ref[...])
        q_ref[:, base + MLA_NOPE:base + MLA_QK_PAD] = (rope * scale).astype(BF16)


def _mla_q(xb, wdq, g, wuq, cos, sin):
    T, D = xb.shape
    tm = ROW_TILE
    return pl.pallas_call(
        _mla_q_kernel,
        grid=(T // tm,),
        in_specs=[pl.BlockSpec((tm, D), lambda i: (i, 0)), _full(wdq.shape), _full(g.shape), _full(wuq.shape),
                  pl.BlockSpec((tm, LANES), lambda i: (i, 0)), pl.BlockSpec((tm, LANES), lambda i: (i, 0))],
        out_specs=pl.BlockSpec((tm, MLA_HEADS * MLA_QK_PAD), lambda i: (i, 0)),
        out_shape=jax.ShapeDtypeStruct((T, MLA_HEADS * MLA_QK_PAD), BF16),
        compiler_params=_params("parallel"),
    )(xb, wdq, g, wuq, cos, sin)


def _attn_kernel(q_ref, k_ref, v_ref, o_ref, m_ref, l_ref, acc_ref):
    i = pl.program_id(1)
    tq = q_ref.shape[0]
    q = q_ref[...]
    m_ref[...] = jnp.full(m_ref.shape, NEG_INF, F32)
    l_ref[...] = jnp.zeros_like(l_ref)
    acc_ref[...] = jnp.zeros_like(acc_ref)

    def step(j, masked):
        start = pl.multiple_of(j * tq, tq)
        k = k_ref[pl.ds(start, tq), :]
        v = v_ref[pl.ds(start, tq), :]
        s = _dot_nt(q, k)
        if masked:
            row = lax.broadcasted_iota(jnp.int32, s.shape, 0)
            col = lax.broadcasted_iota(jnp.int32, s.shape, 1)
            s = jnp.where(col <= row, s, NEG_INF)
        m_prev = m_ref[...]
        m_new = jnp.maximum(m_prev, jnp.max(s, axis=1, keepdims=True))
        p = jnp.exp(s - m_new)
        alpha = jnp.exp(m_prev - m_new)
        l_ref[...] = alpha * l_ref[...] + jnp.sum(p, axis=1, keepdims=True)
        acc_ref[...] = alpha * acc_ref[...] + _dot(p.astype(BF16), v)
        m_ref[...] = m_new

    def body(j, carry):
        step(j, False)
        return carry

    lax.fori_loop(0, i, body, 0)
    step(i, True)
    o_ref[...] = (acc_ref[...] / l_ref[...]).astype(o_ref.dtype)


def _attention(q, k, v):
    T = q.shape[0]
    tq = ATTN_BLOCK
    return pl.pallas_call(
        _attn_kernel,
        grid=(MLA_HEADS, T // tq),
        in_specs=[pl.BlockSpec((tq, MLA_QK_PAD), lambda h, i: (i, h)),
                  pl.BlockSpec((T, MLA_QK_PAD), lambda h, i: (0, h)),
                  pl.BlockSpec((T, MLA_V), lambda h, i: (0, h))],
        out_specs=pl.BlockSpec((tq, MLA_V), lambda h, i: (i, h)),
        out_shape=jax.ShapeDtypeStruct((T, MLA_HEADS * MLA_V), BF16),
        scratch_shapes=[pltpu.VMEM((tq, 1), F32), pltpu.VMEM((tq, 1), F32), pltpu.VMEM((tq, MLA_V), F32)],
        compiler_params=_params("parallel", "arbitrary"),
    )(q, k, v)


def _pad_last(w, width):
    return jnp.pad(w, [(0, 0)] * (w.ndim - 1) + [(0, width - w.shape[-1])])


def _rotate_half_cols(w):
    half = w.shape[-1] // 2
    return jnp.concatenate([-w[..., half:], w[..., :half]], axis=-1)


def _mlstm_weights(w_in, b_if):
    D = D_MODEL
    H = ML_HEADS
    q_end = H * ML_QK
    k_end = 2 * q_end
    v_end = k_end + H * ML_V
    i_end = v_end + H
    f_end = i_end + H
    wq = _pad_last(w_in[:, :q_end].reshape(D, H, ML_QK), LANES).reshape(D, H * LANES)
    wk = _pad_last((w_in[:, q_end:k_end] * (ML_QK ** -0.5)).reshape(D, H, ML_QK), LANES).reshape(D, H * LANES)
    wqv = jnp.concatenate([wq, w_in[:, k_end:v_end]], axis=1).astype(BF16)
    wog = w_in[:, f_end:].astype(BF16)
    wgate = w_in[:, v_end:f_end]
    wg = _pad_last(wgate, LANES).astype(BF16)
    wkt = wk.T.astype(BF16)
    wgt = wgate.T.astype(BF16)
    bias = jnp.concatenate([b_if[0], b_if[1]]).astype(F32)
    bias_cols = _pad_last(bias[None, :], LANES)
    bias_rows = jnp.broadcast_to(bias[:, None], (2 * H, ML_CHUNK))
    return wqv, wog, wg, wkt, wgt, bias_cols, bias_rows


def _mla_weights(kv_w_down, kv_w_up, w_uq):
    H = MLA_HEADS
    wr = kv_w_down[:, MLA_KV_RANK:]
    wd = jnp.concatenate([kv_w_down[:, :MLA_KV_RANK], _pad_last(wr, LANES),
                          _pad_last(_rotate_half_cols(wr), LANES)], axis=1).astype(BF16)
    up = kv_w_up.reshape(MLA_KV_RANK, H, MLA_NOPE + MLA_V)
    wkn = up[:, :, :MLA_NOPE].reshape(MLA_KV_RANK, H * MLA_NOPE).astype(BF16)
    wv = up[:, :, MLA_NOPE:].reshape(MLA_KV_RANK, H * MLA_V).astype(BF16)
    uq = w_uq.reshape(MLA_Q_RANK, H, MLA_NOPE + MLA_ROPE)
    w_a = _pad_last(uq, MLA_QK_PAD).reshape(MLA_Q_RANK, H * MLA_QK_PAD)
    w_r = _pad_last(_rotate_half_cols(uq[:, :, MLA_NOPE:]), LANES).reshape(MLA_Q_RANK, H * LANES)
    wuq = jnp.concatenate([w_a, w_r], axis=1).astype(BF16)
    return wd, wkn, wv, wuq


def _rope_tables(positions):
    inv_freq = ROPE_THETA ** (-jnp.arange(0, MLA_ROPE, 2, dtype=F32) / MLA_ROPE)
    ang = positions.astype(F32)[:, None] * inv_freq
    cos = jnp.cos(ang)
    sin = jnp.sin(ang)
    return (_pad_last(jnp.concatenate([cos, cos], axis=-1), LANES),
            _pad_last(jnp.concatenate([sin, sin], axis=-1), LANES))


def _peer_block(x, xb, p, w_q, sub_keys, u, v, ln_g, ln_b, w_gate, w_proj):
    wq = w_q.astype(BF16)
    keys = sub_keys.reshape(2 * PEER_HEADS, PEER_KEYS, PEER_HALF).astype(BF16)
    dm, w2, c1, g1 = _peer_route(x, wq, keys)
    ffn = _peer_expert(xb, u.astype(BF16), v.T.astype(BF16), dm, w2, c1, g1)
    return _ffn_ln_ple(x, ffn, p, ln_g[None, :], ln_b[None, :], w_gate.astype(BF16), w_proj.astype(BF16))


def kernel(x, p, positions, ln_g, ln_b, a_w_in, a_b_if, a_hn_g, a_w_out, kv_w_down, kv_norm_g, kv_w_up,
           b_w_dq, b_q_norm_g, b_w_uq, b_w_out, peer_w_q, peer_sub_keys, peer_u, peer_v, ple_w_proj,
           ple_w_gate):
    B, S, D = x.shape
    xs = x.reshape(B * S, D)
    ps = p.reshape(DEPTH, B * S, PLE_DIM)
    cos, sin = _rope_tables(positions.reshape(B * S))

    wqv, wog, wg, wkt, wgt, bias_cols, bias_rows = _mlstm_weights(a_w_in[0], a_b_if[0])
    qv, og, g, kt, gt = _mlstm_proj(xs, wqv, wog, wg, wkt, wgt)
    h = _mlstm_scan(qv, kt, g, gt, og, bias_cols, bias_rows, a_hn_g[0][None, :])
    xs = _mix_ln(h, a_w_out[0].astype(BF16), xs, ln_g[0, 0][None, :], ln_b[0, 0][None, :])
    xs, xb = _peer_block(xs, _to_bf16(xs), ps[0], peer_w_q[0], peer_sub_keys[0], peer_u[0], peer_v[0],
                         ln_g[0, 1], ln_b[0, 1], ple_w_gate[0], ple_w_proj[0])

    wd, wkn, wv, wuq = _mla_weights(kv_w_down, kv_w_up, b_w_uq[0])
    k_all, v_all = _mla_kv(xb, wd, kv_norm_g[None, :], wkn, wv, cos, sin)
    q_all = _mla_q(xb, b_w_dq[0].astype(BF16), b_q_norm_g[0][None, :], wuq, cos, sin)
    o = _attention(q_all, k_all, v_all)
    xs = _mix_ln(o, b_w_out[0].astype(BF16), xs, ln_g[1, 0][None, :], ln_b[1, 0][None, :])
    xs, _ = _peer_block(xs, _to_bf16(xs), ps[1], peer_w_q[1], peer_sub_keys[1], peer_u[1], peer_v[1],
                        ln_g[1, 1], ln_b[1, 1], ple_w_gate[1], ple_w_proj[1])
    return xs.reshape(B, S, D)
```

```python
import functools
import math

import jax
import jax.numpy as jnp
from jax import lax
from jax.experimental import pallas as pl
from jax.experimental.pallas import tpu as pltpu

F32 = jnp.float32
BF16 = jnp.bfloat16

LANES = 128
D_MODEL = 1024
DEPTH = 2
DN_ALPHA = (2.0 * DEPTH) ** 0.25
NORM_EPS = 1e-5

ML_HEADS = 8
ML_QK = 64
ML_V = 128
ML_CHUNK = 256

MLA_HEADS = 8
MLA_NOPE = 128
MLA_ROPE = 64
MLA_V = 128
MLA_KV_RANK = 256
MLA_Q_RANK = 384
MLA_QK_PAD = 256
ROPE_THETA = 10000.0
ATTN_BLOCK = 512

PEER_HEADS = 8
PEER_KEYS = 128
PEER_HALF = 128
PEER_TOPK = 16
PLE_DIM = 256

ROW_TILE = 512
ROUTE_TILE = 256
EXPERT_TOK_TILE = 512
EXPERT_TILE = 1024
VMEM_LIMIT = 56 * 1024 * 1024

NEG_INF = float("-inf")


def _params(*sem):
    return pltpu.CompilerParams(dimension_semantics=sem, vmem_limit_bytes=VMEM_LIMIT)


def _dot(a, b):
    return jnp.dot(a, b, preferred_element_type=F32)


def _dot_nt(a, b):
    return lax.dot_general(a, b, (((1,), (1,)), ((), ())), preferred_element_type=F32)


def _layer_norm(y, g, b):
    mu = jnp.mean(y, axis=-1, keepdims=True)
    yc = y - mu
    var = jnp.mean(yc * yc, axis=-1, keepdims=True)
    return yc * lax.rsqrt(var + NORM_EPS) * g + b


def _rms_norm(y, g):
    return y * lax.rsqrt(jnp.mean(y * y, axis=-1, keepdims=True) + NORM_EPS) * g


def _full(shape):
    nd = len(shape)
    return pl.BlockSpec(shape, lambda *_: (0,) * nd)


def _mlstm_proj_kernel(x_ref, wqv_ref, wog_ref, wg_ref, wkt_ref, wgt_ref,
                       qv_ref, og_ref, g_ref, kt_ref, gt_ref):
    xb = x_ref[...].astype(BF16)
    qv_ref[...] = _dot(xb, wqv_ref[...]).astype(BF16)
    og_ref[...] = _dot(xb, wog_ref[...])
    g_ref[...] = _dot(xb, wg_ref[...])
    kt_ref[...] = _dot_nt(wkt_ref[...], xb).astype(BF16)
    gt_ref[...] = _dot_nt(wgt_ref[...], xb)


def _mlstm_proj(x, wqv, wog, wg, wkt, wgt):
    T, D = x.shape
    tm = ROW_TILE
    nqv, nk, ng = wqv.shape[1], wkt.shape[0], wgt.shape[0]
    return pl.pallas_call(
        _mlstm_proj_kernel,
        grid=(T // tm,),
        in_specs=[pl.BlockSpec((tm, D), lambda i: (i, 0)),
                  _full(wqv.shape), _full(wog.shape), _full(wg.shape), _full(wkt.shape), _full(wgt.shape)],
        out_specs=[pl.BlockSpec((tm, nqv), lambda i: (i, 0)),
                   pl.BlockSpec((tm, D), lambda i: (i, 0)),
                   pl.BlockSpec((tm, LANES), lambda i: (i, 0)),
                   pl.BlockSpec((nk, tm), lambda i: (0, i)),
                   pl.BlockSpec((ng, tm), lambda i: (0, i))],
        out_shape=[jax.ShapeDtypeStruct((T, nqv), BF16),
                   jax.ShapeDtypeStruct((T, D), F32),
                   jax.ShapeDtypeStruct((T, LANES), F32),
                   jax.ShapeDtypeStruct((nk, T), BF16),
                   jax.ShapeDtypeStruct((ng, T), F32)],
        compiler_params=_params("parallel"),
    )(x, wqv, wog, wg, wkt, wgt)


def _split3(x):
    hi = x.astype(BF16)
    r1 = x - hi.astype(F32)
    mid = r1.astype(BF16)
    lo = (r1 - mid.astype(F32)).astype(BF16)
    return hi, mid, lo


def _log_sigmoid(x):
    return jnp.minimum(x, 0.0) - jnp.log1p(jnp.exp(-jnp.abs(x)))


def _mlstm_scan_kernel(q_ref, v_ref, kt_ref, g_ref, gt_ref, og_ref, bc_ref, br_ref, hng_ref,
                       o_ref, c_ref, m_ref):
    H = ML_HEADS
    L = q_ref.shape[0]

    @pl.when(pl.program_id(0) == 0)
    def _():
        c_ref[...] = jnp.zeros_like(c_ref)
        m_ref[...] = jnp.zeros_like(m_ref)

    row = lax.broadcasted_iota(jnp.int32, (L, L), 0)
    col = lax.broadcasted_iota(jnp.int32, (L, L), 1)
    causal = col <= row
    tril = jnp.where(causal, 1.0, 0.0).astype(BF16)
    triu = jnp.where(row <= col, 1.0, 0.0).astype(BF16)

    g = g_ref[...] + bc_ref[...]
    gt = gt_ref[...] + br_ref[...]
    b_cols = sum(_dot(tril, piece) for piece in _split3(_log_sigmoid(g)))
    b_rows = sum(_dot(piece, triu) for piece in _split3(_log_sigmoid(gt)))

    lane = lax.broadcasted_iota(jnp.int32, (L, LANES), 1)
    ones_col = jnp.where(lane == 0, 1.0, 0.0).astype(BF16)

    for h in range(H):
        b_c = b_cols[:, H + h:H + h + 1]
        b_r = b_rows[H + h:H + h + 1, :]
        ig_r = gt[h:h + 1, :]
        m_prev = m_ref[h:h + 1, 0:1]
        dmat = jnp.where(causal, b_c - b_r + ig_r, NEG_INF)
        log_inter = b_c + m_prev
        m_t = jnp.maximum(log_inter, jnp.max(dmat, axis=1, keepdims=True))
        w = jnp.exp(dmat - m_t)
        s_inter = jnp.exp(log_inter - m_t)

        q = q_ref[:, h * LANES:(h + 1) * LANES]
        kt = kt_ref[h * LANES:(h + 1) * LANES, :]
        v_ext = jnp.concatenate([v_ref[:, h * ML_V:(h + 1) * ML_V], ones_col], axis=1)
        s = (_dot(q, kt) * w).astype(BF16)
        c_prev = c_ref[h]
        nd = _dot(s, v_ext) + s_inter * _dot(q, c_prev.astype(BF16))
        num = nd[:, :ML_V]
        den = nd[:, ML_V:ML_V + 1]
        hh = num / jnp.maximum(jnp.abs(den), jnp.exp(-m_t))

        b_last = b_r[:, L - 1:L]
        log_src = b_last - b_r + ig_r
        m_new = jnp.maximum(b_last + m_prev, jnp.max(log_src, axis=1, keepdims=True))
        w_src = jnp.exp(log_src - m_new)
        s_old = jnp.exp(b_last + m_prev - m_new)
        c_ref[h] = s_old * c_prev + _dot((kt.astype(F32) * w_src).astype(BF16), v_ext)
        m_ref[h:h + 1, :] = jnp.broadcast_to(m_new, (1, LANES))

        mu = jnp.mean(hh, axis=-1, keepdims=True)
        hc = hh - mu
        var = jnp.mean(hc * hc, axis=-1, keepdims=True)
        hn = hc * lax.rsqrt(var + NORM_EPS) * hng_ref[:, h * ML_V:(h + 1) * ML_V]
        og = og_ref[:, h * ML_V:(h + 1) * ML_V]
        o_ref[:, h * ML_V:(h + 1) * ML_V] = (jax.nn.sigmoid(og) * hn).astype(o_ref.dtype)


def _mlstm_scan(qv, kt, g, gt, og, bias_cols, bias_rows, hn_g):
    T = qv.shape[0]
    L = ML_CHUNK
    D = D_MODEL
    return pl.pallas_call(
        _mlstm_scan_kernel,
        grid=(T // L,),
        in_specs=[pl.BlockSpec((L, D), lambda c: (c, 0)),
                  pl.BlockSpec((L, D), lambda c: (c, 1)),
                  pl.BlockSpec((D, L), lambda c: (0, c)),
                  pl.BlockSpec((L, LANES), lambda c: (c, 0)),
                  pl.BlockSpec((2 * ML_HEADS, L), lambda c: (0, c)),
                  pl.BlockSpec((L, D), lambda c: (c, 0)),
                  _full(bias_cols.shape), _full(bias_rows.shape), _full(hn_g.shape)],
        out_specs=pl.BlockSpec((L, D), lambda c: (c, 0)),
        out_shape=jax.ShapeDtypeStruct((T, D), BF16),
        scratch_shapes=[pltpu.VMEM((ML_HEADS, LANES, 2 * LANES), F32),
                        pltpu.VMEM((ML_HEADS, LANES), F32)],
        compiler_params=_params("arbitrary"),
    )(qv, qv, kt, g, gt, og, bias_cols, bias_rows, hn_g)


def _mix_ln_kernel(a_ref, w_ref, x_ref, g_ref, b_ref, o_ref, *, a_feature_major):
    if a_feature_major:
        mix = lax.dot_general(a_ref[...], w_ref[...], (((0,), (0,)), ((), ())), preferred_element_type=F32)
    else:
        mix = _dot(a_ref[...], w_ref[...])
    o_ref[...] = _layer_norm(mix + DN_ALPHA * x_ref[...], g_ref[...], b_ref[...])


def _mix_ln(a, w, x, g, b, *, a_feature_major=False):
    T, D = x.shape
    tm = ROW_TILE
    if a_feature_major:
        a_spec = pl.BlockSpec((a.shape[0], tm), lambda i: (0, i))
    else:
        a_spec = pl.BlockSpec((tm, a.shape[1]), lambda i: (i, 0))
    return pl.pallas_call(
        functools.partial(_mix_ln_kernel, a_feature_major=a_feature_major),
        grid=(T // tm,),
        in_specs=[a_spec, _full(w.shape),
                  pl.BlockSpec((tm, D), lambda i: (i, 0)), _full(g.shape), _full(b.shape)],
        out_specs=pl.BlockSpec((tm, D), lambda i: (i, 0)),
        out_shape=jax.ShapeDtypeStruct((T, D), F32),
        compiler_params=_params("parallel"),
    )(a, w, x, g, b)


PEER_NTOP = PEER_TOPK + 1
PEER_TOP_ROWS = 24


def _peer_candidate_blocks(a_ref, b_ref):
    blocks = [a_ref[...] + b_ref[0:1, :], b_ref[...] + a_ref[0:1, :]]
    for j in range(1, 8):
        blocks.append(a_ref[0:8, :] + b_ref[j:j + 1, :])
    return jnp.concatenate(blocks, axis=0)


def _bf16_pair_words(x):
    bits = lax.bitcast_convert_type(x, jnp.uint32)
    r = (bits + jnp.uint32(0x7FFF) + ((bits >> 16) & jnp.uint32(1))) >> 16
    return r | (r << 16)


def _peer_route_kernel(x_ref, wq_ref, keys_ref, rk_ref, w2_ref, j1_ref, g1_ref, top_ref):
    xb = x_ref[...].astype(BF16)
    q = _dot(xb, wq_ref[...]).astype(BF16)
    top_ref[...] = jnp.full(top_ref.shape, NEG_INF, F32)
    scores = []
    ranks = []
    for hc in range(2 * PEER_HEADS):
        s = _dot_nt(keys_ref[hc], q[:, hc * PEER_HALF:(hc + 1) * PEER_HALF])
        scores.append(s)
        cur = s
        rank = jnp.full(s.shape, float(PEER_KEYS - 1), F32)
        for r in range(PEER_NTOP):
            mx = jnp.max(cur, axis=0, keepdims=True)
            top_ref[hc, r:r + 1, :] = mx
            hit = cur == mx
            if hc % 2 == 1:
                rank = jnp.where(hit, float(r), rank)
            cur = jnp.where(hit, NEG_INF, cur)
        ranks.append(rank)
    for h in range(PEER_HEADS):
        s1, s2 = scores[2 * h], scores[2 * h + 1]
        cand = _peer_candidate_blocks(top_ref.at[2 * h], top_ref.at[2 * h + 1])
        a0 = top_ref[2 * h, 0:1, :]
        b0 = top_ref[2 * h + 1, 0:1, :]
        best = a0 + b0
        z = jnp.zeros_like(best)
        t_prev = best
        t_cur = best
        for r in range(PEER_NTOP):
            t_prev = t_cur
            t_cur = jnp.max(cand, axis=0, keepdims=True)
            if r < PEER_TOPK:
                z = z + jnp.exp(t_cur - best)
            cand = jnp.where(cand == t_cur, NEG_INF, cand)
        tau = 0.5 * (t_prev + t_cur)
        thr = tau - s1
        count = jnp.zeros_like(s1)
        for r in range(PEER_NTOP):
            count = count + jnp.where(top_ref[2 * h + 1, r:r + 1, :] >= thr, 1.0, 0.0)
        rk_ref[h] = pltpu.bitcast(ranks[2 * h + 1].astype(BF16), jnp.uint32)
        w2_ref[h] = pltpu.bitcast(jnp.exp(s2 - b0).astype(BF16), jnp.uint32)
        j1_ref[h] = _bf16_pair_words(count)
        g1_ref[h] = _bf16_pair_words(jnp.exp(s1 - a0) / z)


def _peer_route(x, wq, keys):
    T, D = x.shape
    tt = ROUTE_TILE
    ospec = pl.BlockSpec((PEER_HEADS, PEER_KEYS, tt), lambda i: (0, 0, i))
    return pl.pallas_call(
        _peer_route_kernel,
        grid=(T // tt,),
        in_specs=[pl.BlockSpec((tt, D), lambda i: (i, 0)), _full(wq.shape), _full(keys.shape)],
        out_specs=[pl.BlockSpec((PEER_HEADS, PEER_KEYS // 2, tt), lambda i: (0, 0, i))] * 2 + [ospec] * 2,
        out_shape=[jax.ShapeDtypeStruct((PEER_HEADS, PEER_KEYS // 2, T), jnp.uint32)] * 2
        + [jax.ShapeDtypeStruct((PEER_HEADS, PEER_KEYS, T), jnp.uint32)] * 2,
        scratch_shapes=[pltpu.VMEM((2 * PEER_HEADS, PEER_TOP_ROWS, tt), F32)],
        compiler_params=_params("parallel"),
    )(x, wq, keys)


def _gelu(x):
    return 0.5 * x * (1.0 + lax.erf(x * (1.0 / math.sqrt(2.0))))


GATE_ROWS = 64


def _row_bf16(words, rows):
    return pltpu.bitcast(jnp.broadcast_to(words, (rows // 2, words.shape[1])), BF16)


def _peer_expert_kernel(x_ref, u_ref, vt_ref, rk_ref, w2_ref, j1_ref, g1_ref, o_ref, acc_ref, act_ref, p_ref):
    j = pl.program_id(1)
    tt = x_ref.shape[0]
    n_sub = u_ref.shape[0] // PEER_KEYS
    n_rb = PEER_KEYS // GATE_ROWS

    @pl.when(j == 0)
    def _():
        acc_ref[...] = jnp.zeros_like(acc_ref)

    act_ref[...] = _dot_nt(u_ref[...], x_ref[...])

    def gate_block(idx, carry):
        rb = idx % n_rb
        ls = pl.ds(pl.multiple_of((idx // n_rb) * LANES, LANES), LANES)
        ws = pl.ds(pl.multiple_of(rb * (GATE_ROWS // 2), GATE_ROWS // 2), GATE_ROWS // 2)
        gates = [jnp.zeros((GATE_ROWS, LANES), BF16) for _ in range(n_sub)]
        for h in range(PEER_HEADS):
            rk = pltpu.bitcast(rk_ref[h, ws, ls], BF16)
            w = pltpu.bitcast(w2_ref[h, ws, ls], BF16)
            for a in range(n_sub):
                count = _row_bf16(j1_ref[h, a:a + 1, ls], GATE_ROWS)
                g = _row_bf16(g1_ref[h, a:a + 1, ls], GATE_ROWS)
                gates[a] = gates[a] + jnp.where(rk < count, w, jnp.zeros_like(w)) * g
        for a in range(n_sub):
            es = pl.ds(pl.multiple_of(a * PEER_KEYS + rb * GATE_ROWS, GATE_ROWS), GATE_ROWS)
            p_ref[es, ls] = gates[a] * _gelu(act_ref[es, ls]).astype(BF16)
        return carry

    lax.fori_loop(0, (tt // LANES) * n_rb, gate_block, 0)
    acc_ref[...] += _dot(vt_ref[...], p_ref[...])

    @pl.when(j == pl.num_programs(1) - 1)
    def _():
        o_ref[...] = acc_ref[...].T


def _peer_expert(xb, u, vt, rk, w2, j1, g1):
    T, D = xb.shape
    E = u.shape[0]
    tt, et = EXPERT_TOK_TILE, EXPERT_TILE
    rspec = pl.BlockSpec((PEER_HEADS, PEER_KEYS // 2, tt), lambda i, j: (0, 0, i))
    sspec = pl.BlockSpec((PEER_HEADS, et // PEER_KEYS, tt), lambda i, j: (0, j, i))
    return pl.pallas_call(
        _peer_expert_kernel,
        grid=(T // tt, E // et),
        in_specs=[pl.BlockSpec((tt, D), lambda i, j: (i, 0)),
                  pl.BlockSpec((et, D), lambda i, j: (j, 0)),
                  pl.BlockSpec((D, et), lambda i, j: (0, j)),
                  rspec, rspec, sspec, sspec],
        out_specs=pl.BlockSpec((tt, D), lambda i, j: (i, 0)),
        out_shape=jax.ShapeDtypeStruct((T, D), F32),
        scratch_shapes=[pltpu.VMEM((D, tt), F32), pltpu.VMEM((et, tt), F32), pltpu.VMEM((et, tt), BF16)],
        compiler_params=_params("parallel", "arbitrary"),
    )(xb, u, vt, rk, w2, j1, g1)


def _ffn_ln_ple_kernel(x_ref, f_ref, p_ref, g_ref, b_ref, wg_ref, wp_ref, o_ref, ob_ref):
    x2 = _layer_norm(DN_ALPHA * x_ref[...] + f_ref[...], g_ref[...], b_ref[...])
    gate = jax.nn.sigmoid(_dot(x2.astype(BF16), wg_ref[...]))
    pe = _dot(p_ref[...].astype(BF16), wp_ref[...])
    out = x2 + gate * pe
    o_ref[...] = out
    ob_ref[...] = out.astype(BF16)


def _ffn_ln_ple(x, f, p, g, b, wg, wp):
    T, D = x.shape
    tm = ROW_TILE
    return pl.pallas_call(
        _ffn_ln_ple_kernel,
        grid=(T // tm,),
        in_specs=[pl.BlockSpec((tm, D), lambda i: (i, 0)), pl.BlockSpec((tm, D), lambda i: (i, 0)),
                  pl.BlockSpec((tm, p.shape[1]), lambda i: (i, 0)),
                  _full(g.shape), _full(b.shape), _full(wg.shape), _full(wp.shape)],
        out_specs=[pl.BlockSpec((tm, D), lambda i: (i, 0))] * 2,
        out_shape=[jax.ShapeDtypeStruct((T, D), F32), jax.ShapeDtypeStruct((T, D), BF16)],
        compiler_params=_params("parallel"),
    )(x, f, p, g, b, wg, wp)


def _cast_kernel(x_ref, o_ref):
    o_ref[...] = x_ref[...].astype(o_ref.dtype)


def _to_bf16(x):
    T, D = x.shape
    tm = ROW_TILE
    return pl.pallas_call(
        _cast_kernel, grid=(T // tm,),
        in_specs=[pl.BlockSpec((tm, D), lambda i: (i, 0))],
        out_specs=pl.BlockSpec((tm, D), lambda i: (i, 0)),
        out_shape=jax.ShapeDtypeStruct((T, D), BF16),
        compiler_params=_params("parallel"),
    )(x)


def _mla_kv_kernel(x_ref, wd_ref, g_ref, wkn_ref, wvt_ref, cos_ref, sin_ref, k_ref, vt_ref):
    ckr = _dot(x_ref[...], wd_ref[...])
    ckv = _rms_norm(ckr[:, :MLA_KV_RANK], g_ref[...]).astype(BF16)
    kn = _dot(ckv, wkn_ref[...])
    vt_ref[...] = _dot_nt(wvt_ref[...], ckv).astype(BF16)
    kr = (ckr[:, MLA_KV_RANK:MLA_KV_RANK + LANES] * cos_ref[...]
          + ckr[:, MLA_KV_RANK + LANES:MLA_KV_RANK + 2 * LANES] * sin_ref[...]).astype(BF16)
    for h in range(MLA_HEADS):
        k_ref[:, h * MLA_QK_PAD:h * MLA_QK_PAD + MLA_NOPE] = kn[:, h * MLA_NOPE:(h + 1) * MLA_NOPE].astype(BF16)
        k_ref[:, h * MLA_QK_PAD + MLA_NOPE:(h + 1) * MLA_QK_PAD] = kr


def _mla_kv(xb, wd, g, wkn, wvt, cos, sin):
    T, D = xb.shape
    tm = ROW_TILE
    return pl.pallas_call(
        _mla_kv_kernel,
        grid=(T // tm,),
        in_specs=[pl.BlockSpec((tm, D), lambda i: (i, 0)), _full(wd.shape), _full(g.shape),
                  _full(wkn.shape), _full(wvt.shape),
                  pl.BlockSpec((tm, LANES), lambda i: (i, 0)), pl.BlockSpec((tm, LANES), lambda i: (i, 0))],
        out_specs=[pl.BlockSpec((tm, MLA_HEADS * MLA_QK_PAD), lambda i: (i, 0)),
                   pl.BlockSpec((MLA_HEADS * MLA_V, tm), lambda i: (0, i))],
        out_shape=[jax.ShapeDtypeStruct((T, MLA_HEADS * MLA_QK_PAD), BF16),
                   jax.ShapeDtypeStruct((MLA_HEADS * MLA_V, T), BF16)],
        compiler_params=_params("parallel"),
    )(xb, wd, g, wkn, wvt, cos, sin)


def _mla_q_kernel(x_ref, wdq_ref, g_ref, wuqt_ref, cost_ref, sint_ref, qt_ref):
    scale = (MLA_NOPE + MLA_ROPE) ** -0.5 * math.log2(math.e)
    cq = _rms_norm(_dot(x_ref[...], wdq_ref[...]), g_ref[...]).astype(BF16)
    qa = _dot_nt(wuqt_ref[...], cq)
    rot0 = MLA_HEADS * MLA_QK_PAD
    for h in range(MLA_HEADS):
        base = h * MLA_QK_PAD
        qt_ref[base:base + MLA_NOPE, :] = (qa[base:base + MLA_NOPE, :] * scale).astype(BF16)
        rope = (qa[base + MLA_NOPE:base + MLA_QK_PAD, :] * cost_ref[...]
                + qa[rot0 + h * LANES:rot0 + (h + 1) * LANES, :] * sint_ref[...])
        qt_ref[base + MLA_NOPE:base + MLA_QK_PAD, :] = (rope * scale).astype(BF16)


def _mla_q(xb, wdq, g, wuqt, cost, sint):
    T, D = xb.shape
    tm = ROW_TILE
    return pl.pallas_call(
        _mla_q_kernel,
        grid=(T // tm,),
        in_specs=[pl.BlockSpec((tm, D), lambda i: (i, 0)), _full(wdq.shape), _full(g.shape), _full(wuqt.shape),
                  pl.BlockSpec((LANES, tm), lambda i: (0, i)), pl.BlockSpec((LANES, tm), lambda i: (0, i))],
        out_specs=pl.BlockSpec((MLA_HEADS * MLA_QK_PAD, tm), lambda i: (0, i)),
        out_shape=jax.ShapeDtypeStruct((MLA_HEADS * MLA_QK_PAD, T), BF16),
        compiler_params=_params("parallel"),
    )(xb, wdq, g, wuqt, cost, sint)


def _attn_kernel(qt_ref, k_ref, vt_ref, o_ref, m_ref, l_ref, acc_ref, s_a, s_b):
    i = pl.program_id(1)
    tq = qt_ref.shape[1]
    m_ref[...] = jnp.full(m_ref.shape, NEG_INF, F32)
    l_ref[...] = jnp.zeros_like(l_ref)
    acc_ref[...] = jnp.zeros_like(acc_ref)

    def scores(j, s_ref):
        k = k_ref[pl.ds(pl.multiple_of(j * tq, tq), tq), :]
        s_ref[...] = _dot(k, qt_ref[...])

    def absorb(j, s_ref, masked):
        vt = vt_ref[:, pl.ds(pl.multiple_of(j * tq, tq), tq)]
        s = s_ref[...]
        if masked:
            key = lax.broadcasted_iota(jnp.int32, s.shape, 0)
            qry = lax.broadcasted_iota(jnp.int32, s.shape, 1)
            s = jnp.where(key <= qry, s, NEG_INF)
        m_prev = m_ref[...]
        m_new = jnp.maximum(m_prev, jnp.max(s, axis=0, keepdims=True))
        p = jnp.exp2(s - m_new)
        alpha = jnp.exp2(m_prev - m_new)
        l_ref[...] = alpha * l_ref[...] + jnp.sum(p, axis=0, keepdims=True)
        acc_ref[...] = alpha * acc_ref[...] + _dot(vt, p.astype(BF16))
        m_ref[...] = m_new

    scores(0, s_a)

    def pair(jj, carry):
        scores(2 * jj + 1, s_b)
        absorb(2 * jj, s_a, False)
        scores(2 * jj + 2, s_a)
        absorb(2 * jj + 1, s_b, False)
        return carry

    lax.fori_loop(0, i // 2, pair, 0)

    @pl.when(i % 2 == 0)
    def _():
        absorb(i, s_a, True)

    @pl.when(i % 2 == 1)
    def _():
        scores(i, s_b)
        absorb(i - 1, s_a, False)
        absorb(i, s_b, True)

    o_ref[...] = (acc_ref[...] / l_ref[...]).astype(o_ref.dtype)


def _attention(qt, k, vt):
    T = k.shape[0]
    tq = ATTN_BLOCK
    return pl.pallas_call(
        _attn_kernel,
        grid=(MLA_HEADS, T // tq),
        in_specs=[pl.BlockSpec((MLA_QK_PAD, tq), lambda h, i: (h, i)),
                  pl.BlockSpec((T, MLA_QK_PAD), lambda h, i: (0, h)),
                  pl.BlockSpec((MLA_V, T), lambda h, i: (h, 0))],
        out_specs=pl.BlockSpec((MLA_V, tq), lambda h, i: (h, i)),
        out_shape=jax.ShapeDtypeStruct((MLA_HEADS * MLA_V, T), BF16),
        scratch_shapes=[pltpu.VMEM((1, tq), F32), pltpu.VMEM((1, tq), F32), pltpu.VMEM((MLA_V, tq), F32),
                        pltpu.VMEM((tq, tq), F32), pltpu.VMEM((tq, tq), F32)],
        compiler_params=_params("parallel", "arbitrary"),
    )(qt, k, vt)


def _pad_last(w, width):
    return jnp.pad(w, [(0, 0)] * (w.ndim - 1) + [(0, width - w.shape[-1])])


def _rotate_half_cols(w):
    half = w.shape[-1] // 2
    return jnp.concatenate([-w[..., half:], w[..., :half]], axis=-1)


def _mlstm_weights(w_in, b_if):
    D = D_MODEL
    H = ML_HEADS
    q_end = H * ML_QK
    k_end = 2 * q_end
    v_end = k_end + H * ML_V
    i_end = v_end + H
    f_end = i_end + H
    wq = _pad_last(w_in[:, :q_end].reshape(D, H, ML_QK), LANES).reshape(D, H * LANES)
    wk = _pad_last((w_in[:, q_end:k_end] * (ML_QK ** -0.5)).reshape(D, H, ML_QK), LANES).reshape(D, H * LANES)
    wqv = jnp.concatenate([wq, w_in[:, k_end:v_end]], axis=1).astype(BF16)
    wog = w_in[:, f_end:].astype(BF16)
    wgate = w_in[:, v_end:f_end]
    wg = _pad_last(wgate, LANES).astype(BF16)
    wkt = wk.T.astype(BF16)
    wgt = wgate.T.astype(BF16)
    bias = jnp.concatenate([b_if[0], b_if[1]]).astype(F32)
    bias_cols = _pad_last(bias[None, :], LANES)
    bias_rows = jnp.broadcast_to(bias[:, None], (2 * H, ML_CHUNK))
    return wqv, wog, wg, wkt, wgt, bias_cols, bias_rows


def _mla_weights(kv_w_down, kv_w_up, w_uq):
    H = MLA_HEADS
    wr = kv_w_down[:, MLA_KV_RANK:]
    wd = jnp.concatenate([kv_w_down[:, :MLA_KV_RANK], _pad_last(wr, LANES),
                          _pad_last(_rotate_half_cols(wr), LANES)], axis=1).astype(BF16)
    up = kv_w_up.reshape(MLA_KV_RANK, H, MLA_NOPE + MLA_V)
    wkn = up[:, :, :MLA_NOPE].reshape(MLA_KV_RANK, H * MLA_NOPE).astype(BF16)
    wvt = up[:, :, MLA_NOPE:].reshape(MLA_KV_RANK, H * MLA_V).T.astype(BF16)
    uq = w_uq.reshape(MLA_Q_RANK, H, MLA_NOPE + MLA_ROPE)
    w_a = _pad_last(uq, MLA_QK_PAD).reshape(MLA_Q_RANK, H * MLA_QK_PAD)
    w_r = _pad_last(_rotate_half_cols(uq[:, :, MLA_NOPE:]), LANES).reshape(MLA_Q_RANK, H * LANES)
    wuqt = jnp.concatenate([w_a, w_r], axis=1).T.astype(BF16)
    return wd, wkn, wvt, wuqt


def _rope_tables(positions):
    inv_freq = ROPE_THETA ** (-jnp.arange(0, MLA_ROPE, 2, dtype=F32) / MLA_ROPE)
    ang = positions.astype(F32)[:, None] * inv_freq
    cos = jnp.cos(ang)
    sin = jnp.sin(ang)
    cos = _pad_last(jnp.concatenate([cos, cos], axis=-1), LANES)
    sin = _pad_last(jnp.concatenate([sin, sin], axis=-1), LANES)
    return cos, sin, cos.T, sin.T


def _peer_block(x, p, w_q, sub_keys, u, v, ln_g, ln_b, w_gate, w_proj):
    wq = w_q.astype(BF16)
    keys = sub_keys.reshape(2 * PEER_HEADS, PEER_KEYS, PEER_HALF).astype(BF16)
    rk, w2, j1, g1 = _peer_route(x, wq, keys)
    ffn = _peer_expert(_to_bf16(x), u.astype(BF16), v.T.astype(BF16), rk, w2, j1, g1)
    return _ffn_ln_ple(x, ffn, p, ln_g[None, :], ln_b[None, :], w_gate.astype(BF16), w_proj.astype(BF16))


def kernel(x, p, positions, ln_g, ln_b, a_w_in, a_b_if, a_hn_g, a_w_out, kv_w_down, kv_norm_g, kv_w_up,
           b_w_dq, b_q_norm_g, b_w_uq, b_w_out, peer_w_q, peer_sub_keys, peer_u, peer_v, ple_w_proj,
           ple_w_gate):
    B, S, D = x.shape
    xs = x.reshape(B * S, D)
    ps = p.reshape(DEPTH, B * S, PLE_DIM)
    cos, sin, cos_t, sin_t = _rope_tables(positions.reshape(B * S))

    wqv, wog, wg, wkt, wgt, bias_cols, bias_rows = _mlstm_weights(a_w_in[0], a_b_if[0])
    qv, og, g, kt, gt = _mlstm_proj(xs, wqv, wog, wg, wkt, wgt)
    h = _mlstm_scan(qv, kt, g, gt, og, bias_cols, bias_rows, a_hn_g[0][None, :])
    xs = _mix_ln(h, a_w_out[0].astype(BF16), xs, ln_g[0, 0][None, :], ln_b[0, 0][None, :])
    xs, xb = _peer_block(xs, ps[0], peer_w_q[0], peer_sub_keys[0], peer_u[0], peer_v[0],
                         ln_g[0, 1], ln_b[0, 1], ple_w_gate[0], ple_w_proj[0])

    wd, wkn, wvt, wuqt = _mla_weights(kv_w_down, kv_w_up, b_w_uq[0])
    k_all, vt_all = _mla_kv(xb, wd, kv_norm_g[None, :], wkn, wvt, cos, sin)
    qt_all = _mla_q(xb, b_w_dq[0].astype(BF16), b_q_norm_g[0][None, :], wuqt, cos_t, sin_t)
    ot = _attention(qt_all, k_all, vt_all)
    xs = _mix_ln(ot, b_w_out[0].astype(BF16), xs, ln_g[1, 0][None, :], ln_b[1, 0][None, :],
                 a_feature_major=True)
    xs, _ = _peer_block(xs, ps[1], peer_w_q[1], peer_sub_keys[1], peer_u[1], peer_v[1],
                        ln_g[1, 1], ln_b[1, 1], ple_w_gate[1], ple_w_proj[1])
    return xs.reshape(B, S, D)
```

```python
import functools
import math

import jax
import jax.numpy as jnp
from jax import lax
from jax.experimental import pallas as pl
from jax.experimental.pallas import tpu as pltpu

F32 = jnp.float32
BF16 = jnp.bfloat16

LANES = 128
D_MODEL = 1024
DEPTH = 2
DN_ALPHA = (2.0 * DEPTH) ** 0.25
NORM_EPS = 1e-5

ML_HEADS = 8
ML_QK = 64
ML_V = 128
ML_CHUNK = 256

MLA_HEADS = 8
MLA_NOPE = 128
MLA_ROPE = 64
MLA_V = 128
MLA_KV_RANK = 256
MLA_Q_RANK = 384
MLA_QK_PAD = 256
ROPE_THETA = 10000.0
ATTN_BLOCK = 1024

PEER_HEADS = 8
PEER_KEYS = 128
PEER_HALF = 128
PEER_TOPK = 16
PLE_DIM = 256

ROW_TILE = 512
ROUTE_TILE = 512
EXPERT_TOK_TILE = 1024
EXPERT_TILE = 1024
VMEM_LIMIT = 56 * 1024 * 1024

NEG_INF = float("-inf")


def _params(*sem):
    return pltpu.CompilerParams(dimension_semantics=sem, vmem_limit_bytes=VMEM_LIMIT)


def _dot(a, b):
    return jnp.dot(a, b, preferred_element_type=F32)


def _dot_nt(a, b):
    return lax.dot_general(a, b, (((1,), (1,)), ((), ())), preferred_element_type=F32)


def _layer_norm(y, g, b):
    mu = jnp.mean(y, axis=-1, keepdims=True)
    yc = y - mu
    var = jnp.mean(yc * yc, axis=-1, keepdims=True)
    return yc * lax.rsqrt(var + NORM_EPS) * g + b


def _rms_norm(y, g):
    return y * lax.rsqrt(jnp.mean(y * y, axis=-1, keepdims=True) + NORM_EPS) * g


def _full(shape):
    nd = len(shape)
    return pl.BlockSpec(shape, lambda *_: (0,) * nd)


def _mlstm_proj_kernel(x_ref, wqv_ref, wog_ref, wg_ref, wkt_ref, wgt_ref,
                       qv_ref, og_ref, g_ref, kt_ref, gt_ref):
    xb = x_ref[...].astype(BF16)
    qv_ref[...] = _dot(xb, wqv_ref[...]).astype(BF16)
    og_ref[...] = _dot(xb, wog_ref[...])
    g_ref[...] = _dot(xb, wg_ref[...])
    kt_ref[...] = _dot_nt(wkt_ref[...], xb).astype(BF16)
    gt_ref[...] = _dot_nt(wgt_ref[...], xb)


def _mlstm_proj(x, wqv, wog, wg, wkt, wgt):
    T, D = x.shape
    tm = ROW_TILE
    nqv, nk, ng = wqv.shape[1], wkt.shape[0], wgt.shape[0]
    return pl.pallas_call(
        _mlstm_proj_kernel,
        grid=(T // tm,),
        in_specs=[pl.BlockSpec((tm, D), lambda i: (i, 0)),
                  _full(wqv.shape), _full(wog.shape), _full(wg.shape), _full(wkt.shape), _full(wgt.shape)],
        out_specs=[pl.BlockSpec((tm, nqv), lambda i: (i, 0)),
                   pl.BlockSpec((tm, D), lambda i: (i, 0)),
                   pl.BlockSpec((tm, LANES), lambda i: (i, 0)),
                   pl.BlockSpec((nk, tm), lambda i: (0, i)),
                   pl.BlockSpec((ng, tm), lambda i: (0, i))],
        out_shape=[jax.ShapeDtypeStruct((T, nqv), BF16),
                   jax.ShapeDtypeStruct((T, D), F32),
                   jax.ShapeDtypeStruct((T, LANES), F32),
                   jax.ShapeDtypeStruct((nk, T), BF16),
                   jax.ShapeDtypeStruct((ng, T), F32)],
        compiler_params=_params("parallel"),
    )(x, wqv, wog, wg, wkt, wgt)


def _split3(x):
    hi = x.astype(BF16)
    r1 = x - hi.astype(F32)
    mid = r1.astype(BF16)
    lo = (r1 - mid.astype(F32)).astype(BF16)
    return hi, mid, lo


def _log_sigmoid(x):
    return jnp.minimum(x, 0.0) - jnp.log1p(jnp.exp(-jnp.abs(x)))


def _mlstm_scan_kernel(q_ref, v_ref, kt_ref, g_ref, gt_ref, og_ref, bc_ref, br_ref, hng_ref,
                       o_ref, c_ref, m_ref):
    H = ML_HEADS
    L = q_ref.shape[0]

    @pl.when(pl.program_id(0) == 0)
    def _():
        c_ref[...] = jnp.zeros_like(c_ref)
        m_ref[...] = jnp.zeros_like(m_ref)

    row = lax.broadcasted_iota(jnp.int32, (L, L), 0)
    col = lax.broadcasted_iota(jnp.int32, (L, L), 1)
    causal = col <= row
    tril = jnp.where(causal, 1.0, 0.0).astype(BF16)
    triu = jnp.where(row <= col, 1.0, 0.0).astype(BF16)

    g = g_ref[...] + bc_ref[...]
    gt = gt_ref[...] + br_ref[...]
    b_cols = sum(_dot(tril, piece) for piece in _split3(_log_sigmoid(g)))
    b_rows = sum(_dot(piece, triu) for piece in _split3(_log_sigmoid(gt)))

    lane = lax.broadcasted_iota(jnp.int32, (L, LANES), 1)
    ones_col = jnp.where(lane == 0, 1.0, 0.0).astype(BF16)

    for h in range(H):
        b_c = b_cols[:, H + h:H + h + 1]
        b_r = b_rows[H + h:H + h + 1, :]
        ig_r = gt[h:h + 1, :]
        m_prev = m_ref[h:h + 1, 0:1]
        dmat = jnp.where(causal, b_c - b_r + ig_r, NEG_INF)
        log_inter = b_c + m_prev
        m_t = jnp.maximum(log_inter, jnp.max(dmat, axis=1, keepdims=True))
        w = jnp.exp(dmat - m_t)
        s_inter = jnp.exp(log_inter - m_t)

        q = q_ref[:, h * LANES:(h + 1) * LANES]
        kt = kt_ref[h * LANES:(h + 1) * LANES, :]
        v_ext = jnp.concatenate([v_ref[:, h * ML_V:(h + 1) * ML_V], ones_col], axis=1)
        s = (_dot(q, kt) * w).astype(BF16)
        c_prev = c_ref[h]
        nd = _dot(s, v_ext) + s_inter * _dot(q, c_prev.astype(BF16))
        num = nd[:, :ML_V]
        den = nd[:, ML_V:ML_V + 1]
        hh = num / jnp.maximum(jnp.abs(den), jnp.exp(-m_t))

        b_last = b_r[:, L - 1:L]
        log_src = b_last - b_r + ig_r
        m_new = jnp.maximum(b_last + m_prev, jnp.max(log_src, axis=1, keepdims=True))
        w_src = jnp.exp(log_src - m_new)
        s_old = jnp.exp(b_last + m_prev - m_new)
        c_ref[h] = s_old * c_prev + _dot((kt.astype(F32) * w_src).astype(BF16), v_ext)
        m_ref[h:h + 1, :] = jnp.broadcast_to(m_new, (1, LANES))

        mu = jnp.mean(hh, axis=-1, keepdims=True)
        hc = hh - mu
        var = jnp.mean(hc * hc, axis=-1, keepdims=True)
        hn = hc * lax.rsqrt(var + NORM_EPS) * hng_ref[:, h * ML_V:(h + 1) * ML_V]
        og = og_ref[:, h * ML_V:(h + 1) * ML_V]
        o_ref[:, h * ML_V:(h + 1) * ML_V] = (jax.nn.sigmoid(og) * hn).astype(o_ref.dtype)


def _mlstm_scan(qv, kt, g, gt, og, bias_cols, bias_rows, hn_g):
    T = qv.shape[0]
    L = ML_CHUNK
    D = D_MODEL
    return pl.pallas_call(
        _mlstm_scan_kernel,
        grid=(T // L,),
        in_specs=[pl.BlockSpec((L, D), lambda c: (c, 0)),
                  pl.BlockSpec((L, D), lambda c: (c, 1)),
                  pl.BlockSpec((D, L), lambda c: (0, c)),
                  pl.BlockSpec((L, LANES), lambda c: (c, 0)),
                  pl.BlockSpec((2 * ML_HEADS, L), lambda c: (0, c)),
                  pl.BlockSpec((L, D), lambda c: (c, 0)),
                  _full(bias_cols.shape), _full(bias_rows.shape), _full(hn_g.shape)],
        out_specs=pl.BlockSpec((L, D), lambda c: (c, 0)),
        out_shape=jax.ShapeDtypeStruct((T, D), BF16),
        scratch_shapes=[pltpu.VMEM((ML_HEADS, LANES, 2 * LANES), F32),
                        pltpu.VMEM((ML_HEADS, LANES), F32)],
        compiler_params=_params("arbitrary"),
    )(qv, qv, kt, g, gt, og, bias_cols, bias_rows, hn_g)


def _mix_ln_kernel(a_ref, w_ref, x_ref, g_ref, b_ref, o_ref, ob_ref, *, a_feature_major):
    if a_feature_major:
        mix = lax.dot_general(a_ref[...], w_ref[...], (((0,), (0,)), ((), ())), preferred_element_type=F32)
    else:
        mix = _dot(a_ref[...], w_ref[...])
    out = _layer_norm(mix + DN_ALPHA * x_ref[...], g_ref[...], b_ref[...])
    o_ref[...] = out
    ob_ref[...] = out.astype(BF16)


def _mix_ln(a, w, x, g, b, *, a_feature_major=False):
    T, D = x.shape
    tm = ROW_TILE
    if a_feature_major:
        a_spec = pl.BlockSpec((a.shape[0], tm), lambda i: (0, i))
    else:
        a_spec = pl.BlockSpec((tm, a.shape[1]), lambda i: (i, 0))
    return pl.pallas_call(
        functools.partial(_mix_ln_kernel, a_feature_major=a_feature_major),
        grid=(T // tm,),
        in_specs=[a_spec, _full(w.shape),
                  pl.BlockSpec((tm, D), lambda i: (i, 0)), _full(g.shape), _full(b.shape)],
        out_specs=[pl.BlockSpec((tm, D), lambda i: (i, 0))] * 2,
        out_shape=[jax.ShapeDtypeStruct((T, D), F32), jax.ShapeDtypeStruct((T, D), BF16)],
        compiler_params=_params("parallel"),
    )(a, w, x, g, b)


PEER_NTOP = PEER_TOPK + 1
PEER_TOP_ROWS = 24


def _peer_candidate_blocks(a_ref, b_ref):
    blocks = [a_ref[...] + b_ref[0:1, :], b_ref[...] + a_ref[0:1, :]]
    for j in range(1, 8):
        blocks.append(a_ref[0:8, :] + b_ref[j:j + 1, :])
    return jnp.concatenate(blocks, axis=0)


def _bf16_pair_words(x):
    bits = lax.bitcast_convert_type(x, jnp.uint32)
    r = (bits + jnp.uint32(0x7FFF) + ((bits >> 16) & jnp.uint32(1))) >> 16
    return r | (r << 16)


def _peer_route_kernel(x_ref, wq_ref, keys_ref, rk_ref, w2_ref, j1_ref, g1_ref, q_ref, top_ref):
    q_ref[...] = _dot(x_ref[...].astype(BF16), wq_ref[...]).astype(BF16)
    top_ref[...] = jnp.full(top_ref.shape, NEG_INF, F32)

    def head(h, carry):
        halves = []
        for c in range(2):
            hc = 2 * h + c
            qs = q_ref[:, hc * PEER_HALF:(hc + 1) * PEER_HALF]
            s = _dot_nt(keys_ref[hc], qs)
            cur = s
            rank = jnp.full(s.shape, float(PEER_KEYS - 1), F32)
            for r in range(PEER_NTOP):
                mx = jnp.max(cur, axis=0, keepdims=True)
                top_ref[hc, r:r + 1, :] = mx
                hit = cur == mx
                if c == 1:
                    rank = jnp.where(hit, float(r), rank)
                cur = jnp.where(hit, NEG_INF, cur)
            halves.append((s, rank))
        (s1, _), (s2, rank2) = halves
        cand = _peer_candidate_blocks(top_ref.at[2 * h], top_ref.at[2 * h + 1])
        a0 = top_ref[2 * h, 0:1, :]
        b0 = top_ref[2 * h + 1, 0:1, :]
        best = a0 + b0
        z = jnp.zeros_like(best)
        t_prev = best
        t_cur = best
        for r in range(PEER_NTOP):
            t_prev = t_cur
            t_cur = jnp.max(cand, axis=0, keepdims=True)
            if r < PEER_TOPK:
                z = z + jnp.exp(t_cur - best)
            cand = jnp.where(cand == t_cur, NEG_INF, cand)
        tau = 0.5 * (t_prev + t_cur)
        thr = tau - s1
        count = jnp.zeros_like(s1)
        for r in range(PEER_NTOP):
            count = jnp.where(top_ref[2 * h + 1, r:r + 1, :] >= thr, float(r + 1), count)
        rk_ref[h] = pltpu.bitcast(rank2.astype(BF16), jnp.uint32)
        w2_ref[h] = pltpu.bitcast(jnp.exp(s2 - b0).astype(BF16), jnp.uint32)
        j1_ref[h] = _bf16_pair_words(count)
        g1_ref[h] = _bf16_pair_words(jnp.exp(s1 - a0) / z)
        return carry

    for h in range(PEER_HEADS):
        head(h, 0)


def _peer_route(x, wq, keys):
    T, D = x.shape
    tt = ROUTE_TILE
    ospec = pl.BlockSpec((PEER_HEADS, PEER_KEYS, tt), lambda i: (0, 0, i))
    return pl.pallas_call(
        _peer_route_kernel,
        grid=(T // tt,),
        in_specs=[pl.BlockSpec((tt, D), lambda i: (i, 0)), _full(wq.shape), _full(keys.shape)],
        out_specs=[pl.BlockSpec((PEER_HEADS, PEER_KEYS // 2, tt), lambda i: (0, 0, i))] * 2 + [ospec] * 2,
        out_shape=[jax.ShapeDtypeStruct((PEER_HEADS, PEER_KEYS // 2, T), jnp.uint32)] * 2
        + [jax.ShapeDtypeStruct((PEER_HEADS, PEER_KEYS, T), jnp.uint32)] * 2,
        scratch_shapes=[pltpu.VMEM((tt, 2 * PEER_HEADS * PEER_HALF), BF16), pltpu.VMEM((2 * PEER_HEADS, PEER_TOP_ROWS, tt), F32)],
        compiler_params=_params("parallel"),
    )(x, wq, keys)


def _gelu(x):
    return 0.5 * x * (1.0 + lax.erf(x * (1.0 / math.sqrt(2.0))))


GATE_ROWS = 64


def _row_bf16(words, rows):
    return pltpu.bitcast(jnp.broadcast_to(words, (rows // 2, words.shape[1])), BF16)


def _peer_expert_kernel(x_ref, u_ref, vt_ref, rk_ref, w2_ref, j1_ref, g1_ref, o_ref,
                        acc_ref, act0, act1, p0, p1, *, n_tiles):
    s = pl.program_id(1)
    last = pl.num_programs(1) - 1
    tt = x_ref.shape[0]
    n_sub = u_ref.shape[0] // PEER_KEYS
    n_rb = PEER_KEYS // GATE_ROWS

    @pl.when(s == 0)
    def _():
        acc_ref[...] = jnp.zeros_like(acc_ref)
        p1[...] = jnp.zeros_like(p1)
        act0[...] = _dot_nt(u_ref[...], x_ref[...])

    def gate_block(act_ref, p_ref, lc, rb):
        ls = slice(lc * LANES, (lc + 1) * LANES)
        ws = slice(rb * (GATE_ROWS // 2), (rb + 1) * (GATE_ROWS // 2))
        gates = [None] * n_sub
        for h in range(PEER_HEADS):
            rk = pltpu.bitcast(rk_ref[h, ws, ls], BF16)
            w = pltpu.bitcast(w2_ref[h, ws, ls], BF16)
            for a in range(n_sub):
                count = _row_bf16(j1_ref[h, a:a + 1, ls], GATE_ROWS)
                g = _row_bf16(g1_ref[h, a:a + 1, ls], GATE_ROWS)
                term = jnp.where(rk < count, w, jnp.zeros_like(w)) * g
                gates[a] = term if h == 0 else gates[a] + term
        for a in range(n_sub):
            es = slice(a * PEER_KEYS + rb * GATE_ROWS, a * PEER_KEYS + (rb + 1) * GATE_ROWS)
            p_ref[es, ls] = gates[a] * _gelu(act_ref[es, ls]).astype(BF16)

    def stages(act_w, act_r, p_w, p_r):
        acc_ref[...] += _dot(vt_ref[...], p_r[...])
        act_w[...] = _dot_nt(u_ref[...], x_ref[...])
        for lc in range(tt // LANES):
            for rb in range(n_rb):
                gate_block(act_r, p_w, lc, rb)

    steady = jnp.logical_and(s > 0, s < last)

    @pl.when(jnp.logical_and(steady, s % 2 == 0))
    def _():
        stages(act0, act1, p1, p0)

    @pl.when(jnp.logical_and(steady, s % 2 == 1))
    def _():
        stages(act1, act0, p0, p1)

    @pl.when(s == last)
    def _():
        p_last = p1 if n_tiles % 2 == 0 else p0
        o_ref[...] = (acc_ref[...] + _dot(vt_ref[...], p_last[...])).T


def _peer_expert(xb, u, vt, rk, w2, j1, g1):
    T, D = xb.shape
    E = u.shape[0]
    tt, et = EXPERT_TOK_TILE, EXPERT_TILE
    n_e = E // et
    rspec = pl.BlockSpec((PEER_HEADS, PEER_KEYS // 2, tt), lambda i, s: (0, 0, i))
    sspec = pl.BlockSpec((PEER_HEADS, et // PEER_KEYS, tt),
                         lambda i, s: (0, jnp.clip(s - 1, 0, n_e - 1), i))
    return pl.pallas_call(
        functools.partial(_peer_expert_kernel, n_tiles=n_e),
        grid=(T // tt, n_e + 2),
        in_specs=[pl.BlockSpec((tt, D), lambda i, s: (i, 0)),
                  pl.BlockSpec((et, D), lambda i, s: (jnp.minimum(s, n_e - 1), 0)),
                  pl.BlockSpec((None, D, et), lambda i, s: (jnp.clip(s - 2, 0, n_e - 1), 0, 0)),
                  rspec, rspec, sspec, sspec],
        out_specs=pl.BlockSpec((tt, D), lambda i, s: (i, 0)),
        out_shape=jax.ShapeDtypeStruct((T, D), F32),
        scratch_shapes=[pltpu.VMEM((D, tt), F32), pltpu.VMEM((et, tt), F32), pltpu.VMEM((et, tt), F32),
                        pltpu.VMEM((et, tt), BF16), pltpu.VMEM((et, tt), BF16)],
        compiler_params=_params("parallel", "arbitrary"),
    )(xb, u, vt, rk, w2, j1, g1)


def _ffn_ln_ple_kernel(x_ref, f_ref, p_ref, g_ref, b_ref, wg_ref, wp_ref, o_ref, ob_ref):
    x2 = _layer_norm(DN_ALPHA * x_ref[...] + f_ref[...], g_ref[...], b_ref[...])
    gate = jax.nn.sigmoid(_dot(x2.astype(BF16), wg_ref[...]))
    pe = _dot(p_ref[...].astype(BF16), wp_ref[...])
    out = x2 + gate * pe
    o_ref[...] = out
    ob_ref[...] = out.astype(BF16)


def _ffn_ln_ple(x, f, p, g, b, wg, wp):
    T, D = x.shape
    tm = ROW_TILE
    return pl.pallas_call(
        _ffn_ln_ple_kernel,
        grid=(T // tm,),
        in_specs=[pl.BlockSpec((tm, D), lambda i: (i, 0)), pl.BlockSpec((tm, D), lambda i: (i, 0)),
                  pl.BlockSpec((tm, p.shape[1]), lambda i: (i, 0)),
                  _full(g.shape), _full(b.shape), _full(wg.shape), _full(wp.shape)],
        out_specs=[pl.BlockSpec((tm, D), lambda i: (i, 0))] * 2,
        out_shape=[jax.ShapeDtypeStruct((T, D), F32), jax.ShapeDtypeStruct((T, D), BF16)],
        compiler_params=_params("parallel"),
    )(x, f, p, g, b, wg, wp)


def _mla_kv_kernel(x_ref, wd_ref, g_ref, wkn_ref, wvt_ref, cos_ref, sin_ref, k_ref, vt_ref):
    ckr = _dot(x_ref[...], wd_ref[...])
    ckv = _rms_norm(ckr[:, :MLA_KV_RANK], g_ref[...]).astype(BF16)
    kn = _dot(ckv, wkn_ref[...])
    vt_ref[...] = _dot_nt(wvt_ref[...], ckv).astype(BF16)
    kr = (ckr[:, MLA_KV_RANK:MLA_KV_RANK + LANES] * cos_ref[...]
          + ckr[:, MLA_KV_RANK + LANES:MLA_KV_RANK + 2 * LANES] * sin_ref[...]).astype(BF16)
    for h in range(MLA_HEADS):
        k_ref[:, h * MLA_QK_PAD:h * MLA_QK_PAD + MLA_NOPE] = kn[:, h * MLA_NOPE:(h + 1) * MLA_NOPE].astype(BF16)
        k_ref[:, h * MLA_QK_PAD + MLA_NOPE:(h + 1) * MLA_QK_PAD] = kr


def _mla_kv(xb, wd, g, wkn, wvt, cos, sin):
    T, D = xb.shape
    tm = ROW_TILE
    return pl.pallas_call(
        _mla_kv_kernel,
        grid=(T // tm,),
        in_specs=[pl.BlockSpec((tm, D), lambda i: (i, 0)), _full(wd.shape), _full(g.shape),
                  _full(wkn.shape), _full(wvt.shape),
                  pl.BlockSpec((tm, LANES), lambda i: (i, 0)), pl.BlockSpec((tm, LANES), lambda i: (i, 0))],
        out_specs=[pl.BlockSpec((tm, MLA_HEADS * MLA_QK_PAD), lambda i: (i, 0)),
                   pl.BlockSpec((MLA_HEADS * MLA_V, tm), lambda i: (0, i))],
        out_shape=[jax.ShapeDtypeStruct((T, MLA_HEADS * MLA_QK_PAD), BF16),
                   jax.ShapeDtypeStruct((MLA_HEADS * MLA_V, T), BF16)],
        compiler_params=_params("parallel"),
    )(xb, wd, g, wkn, wvt, cos, sin)


def _mla_q_kernel(x_ref, wdq_ref, g_ref, wuqt_ref, cost_ref, sint_ref, qt_ref):
    scale = (MLA_NOPE + MLA_ROPE) ** -0.5 * math.log2(math.e)
    cq = _rms_norm(_dot(x_ref[...], wdq_ref[...]), g_ref[...]).astype(BF16)
    qa = _dot_nt(wuqt_ref[...], cq)
    rot0 = MLA_HEADS * MLA_QK_PAD
    for h in range(MLA_HEADS):
        base = h * MLA_QK_PAD
        qt_ref[base:base + MLA_NOPE, :] = (qa[base:base + MLA_NOPE, :] * scale).astype(BF16)
        rope = (qa[base + MLA_NOPE:base + MLA_QK_PAD, :] * cost_ref[...]
                + qa[rot0 + h * LANES:rot0 + (h + 1) * LANES, :] * sint_ref[...])
        qt_ref[base + MLA_NOPE:base + MLA_QK_PAD, :] = (rope * scale).astype(BF16)


def _mla_q(xb, wdq, g, wuqt, cost, sint):
    T, D = xb.shape
    tm = ROW_TILE
    return pl.pallas_call(
        _mla_q_kernel,
        grid=(T // tm,),
        in_specs=[pl.BlockSpec((tm, D), lambda i: (i, 0)), _full(wdq.shape), _full(g.shape), _full(wuqt.shape),
                  pl.BlockSpec((LANES, tm), lambda i: (0, i)), pl.BlockSpec((LANES, tm), lambda i: (0, i))],
        out_specs=pl.BlockSpec((MLA_HEADS * MLA_QK_PAD, tm), lambda i: (0, i)),
        out_shape=jax.ShapeDtypeStruct((MLA_HEADS * MLA_QK_PAD, T), BF16),
        compiler_params=_params("parallel"),
    )(xb, wdq, g, wuqt, cost, sint)


def _attn_kernel(qt_ref, k_ref, vt_ref, o_ref, m_ref, l_ref, acc_ref, s_a, s_b):
    i = pl.program_id(1)
    tq = qt_ref.shape[1]
    m_ref[...] = jnp.full(m_ref.shape, NEG_INF, F32)
    l_ref[...] = jnp.zeros_like(l_ref)
    acc_ref[...] = jnp.zeros_like(acc_ref)

    def scores(j, s_ref):
        k = k_ref[pl.ds(pl.multiple_of(j * tq, tq), tq), :]
        s_ref[...] = _dot(k, qt_ref[...])

    def absorb(j, s_ref, masked):
        vt = vt_ref[:, pl.ds(pl.multiple_of(j * tq, tq), tq)]
        s = s_ref[...]
        if masked:
            key = lax.broadcasted_iota(jnp.int32, s.shape, 0)
            qry = lax.broadcasted_iota(jnp.int32, s.shape, 1)
            s = jnp.where(key <= qry, s, NEG_INF)
        m_prev = m_ref[...]
        m_new = jnp.maximum(m_prev, jnp.max(s, axis=0, keepdims=True))
        p = jnp.exp2(s - m_new)
        alpha = jnp.exp2(m_prev - m_new)
        l_ref[...] = alpha * l_ref[...] + jnp.sum(p, axis=0, keepdims=True)
        acc_ref[...] = alpha * acc_ref[...] + _dot(vt, p.astype(BF16))
        m_ref[...] = m_new

    scores(0, s_a)

    def pair(jj, carry):
        scores(2 * jj + 1, s_b)
        absorb(2 * jj, s_a, False)
        scores(2 * jj + 2, s_a)
        absorb(2 * jj + 1, s_b, False)
        return carry

    lax.fori_loop(0, i // 2, pair, 0)

    @pl.when(i % 2 == 0)
    def _():
        absorb(i, s_a, True)

    @pl.when(i % 2 == 1)
    def _():
        scores(i, s_b)
        absorb(i - 1, s_a, False)
        absorb(i, s_b, True)

    o_ref[...] = (acc_ref[...] / l_ref[...]).astype(o_ref.dtype)


def _attention(qt, k, vt):
    T = k.shape[0]
    tq = ATTN_BLOCK
    return pl.pallas_call(
        _attn_kernel,
        grid=(MLA_HEADS, T // tq),
        in_specs=[pl.BlockSpec((MLA_QK_PAD, tq), lambda h, i: (h, i)),
                  pl.BlockSpec((T, MLA_QK_PAD), lambda h, i: (0, h)),
                  pl.BlockSpec((MLA_V, T), lambda h, i: (h, 0))],
        out_specs=pl.BlockSpec((MLA_V, tq), lambda h, i: (h, i)),
        out_shape=jax.ShapeDtypeStruct((MLA_HEADS * MLA_V, T), BF16),
        scratch_shapes=[pltpu.VMEM((1, tq), F32), pltpu.VMEM((1, tq), F32), pltpu.VMEM((MLA_V, tq), F32),
                        pltpu.VMEM((tq, tq), F32), pltpu.VMEM((tq, tq), F32)],
        compiler_params=_params("parallel", "arbitrary"),
    )(qt, k, vt)


def _pad_last(w, width):
    return jnp.pad(w, [(0, 0)] * (w.ndim - 1) + [(0, width - w.shape[-1])])


def _rotate_half_cols(w):
    half = w.shape[-1] // 2
    return jnp.concatenate([-w[..., half:], w[..., :half]], axis=-1)


def _mlstm_weights(w_in, b_if):
    D = D_MODEL
    H = ML_HEADS
    q_end = H * ML_QK
    k_end = 2 * q_end
    v_end = k_end + H * ML_V
    i_end = v_end + H
    f_end = i_end + H
    wq = _pad_last(w_in[:, :q_end].reshape(D, H, ML_QK), LANES).reshape(D, H * LANES)
    wk = _pad_last((w_in[:, q_end:k_end] * (ML_QK ** -0.5)).reshape(D, H, ML_QK), LANES).reshape(D, H * LANES)
    wqv = jnp.concatenate([wq, w_in[:, k_end:v_end]], axis=1).astype(BF16)
    wog = w_in[:, f_end:].astype(BF16)
    wgate = w_in[:, v_end:f_end]
    wg = _pad_last(wgate, LANES).astype(BF16)
    wkt = wk.T.astype(BF16)
    wgt = wgate.T.astype(BF16)
    bias = jnp.concatenate([b_if[0], b_if[1]]).astype(F32)
    bias_cols = _pad_last(bias[None, :], LANES)
    bias_rows = jnp.broadcast_to(bias[:, None], (2 * H, ML_CHUNK))
    return wqv, wog, wg, wkt, wgt, bias_cols, bias_rows


def _mla_weights(kv_w_down, kv_w_up, w_uq):
    H = MLA_HEADS
    wr = kv_w_down[:, MLA_KV_RANK:]
    wd = jnp.concatenate([kv_w_down[:, :MLA_KV_RANK], _pad_last(wr, LANES),
                          _pad_last(_rotate_half_cols(wr), LANES)], axis=1).astype(BF16)
    up = kv_w_up.reshape(MLA_KV_RANK, H, MLA_NOPE + MLA_V)
    wkn = up[:, :, :MLA_NOPE].reshape(MLA_KV_RANK, H * MLA_NOPE).astype(BF16)
    wvt = up[:, :, MLA_NOPE:].reshape(MLA_KV_RANK, H * MLA_V).T.astype(BF16)
    uq = w_uq.reshape(MLA_Q_RANK, H, MLA_NOPE + MLA_ROPE)
    w_a = _pad_last(uq, MLA_QK_PAD).reshape(MLA_Q_RANK, H * MLA_QK_PAD)
    w_r = _pad_last(_rotate_half_cols(uq[:, :, MLA_NOPE:]), LANES).reshape(MLA_Q_RANK, H * LANES)
    wuqt = jnp.concatenate([w_a, w_r], axis=1).T.astype(BF16)
    return wd, wkn, wvt, wuqt


def _rope_tables(positions):
    inv_freq = ROPE_THETA ** (-jnp.arange(0, MLA_ROPE, 2, dtype=F32) / MLA_ROPE)
    ang = positions.astype(F32)[:, None] * inv_freq
    cos = jnp.cos(ang)
    sin = jnp.sin(ang)
    cos = _pad_last(jnp.concatenate([cos, cos], axis=-1), LANES)
    sin = _pad_last(jnp.concatenate([sin, sin], axis=-1), LANES)
    return cos, sin, cos.T, sin.T


def _peer_block(x, xb, p, w_q, sub_keys, u, v, ln_g, ln_b, w_gate, w_proj):
    wq = w_q.astype(BF16)
    keys = sub_keys.reshape(2 * PEER_HEADS, PEER_KEYS, PEER_HALF).astype(BF16)
    rk, w2, j1, g1 = _peer_route(x, wq, keys)
    vt = jnp.swapaxes(v.astype(BF16).reshape(-1, EXPERT_TILE, v.shape[1]), 1, 2)
    ffn = _peer_expert(xb, u.astype(BF16), vt, rk, w2, j1, g1)
    return _ffn_ln_ple(x, ffn, p, ln_g[None, :], ln_b[None, :], w_gate.astype(BF16), w_proj.astype(BF16))


def kernel(x, p, positions, ln_g, ln_b, a_w_in, a_b_if, a_hn_g, a_w_out, kv_w_down, kv_norm_g, kv_w_up,
           b_w_dq, b_q_norm_g, b_w_uq, b_w_out, peer_w_q, peer_sub_keys, peer_u, peer_v, ple_w_proj,
           ple_w_gate):
    B, S, D = x.shape
    xs = x.reshape(B * S, D)
    ps = p.reshape(DEPTH, B * S, PLE_DIM)
    cos, sin, cos_t, sin_t = _rope_tables(positions.reshape(B * S))

    wqv, wog, wg, wkt, wgt, bias_cols, bias_rows = _mlstm_weights(a_w_in[0], a_b_if[0])
    qv, og, g, kt, gt = _mlstm_proj(xs, wqv, wog, wg, wkt, wgt)
    h = _mlstm_scan(qv, kt, g, gt, og, bias_cols, bias_rows, a_hn_g[0][None, :])
    xs, xb = _mix_ln(h, a_w_out[0].astype(BF16), xs, ln_g[0, 0][None, :], ln_b[0, 0][None, :])
    xs, xb = _peer_block(xs, xb, ps[0], peer_w_q[0], peer_sub_keys[0], peer_u[0], peer_v[0],
                         ln_g[0, 1], ln_b[0, 1], ple_w_gate[0], ple_w_proj[0])

    wd, wkn, wvt, wuqt = _mla_weights(kv_w_down, kv_w_up, b_w_uq[0])
    k_all, vt_all = _mla_kv(xb, wd, kv_norm_g[None, :], wkn, wvt, cos, sin)
    qt_all = _mla_q(xb, b_w_dq[0].astype(BF16), b_q_norm_g[0][None, :], wuqt, cos_t, sin_t)
    ot = _attention(qt_all, k_all, vt_all)
    xs, xb = _mix_ln(ot, b_w_out[0].astype(BF16), xs, ln_g[1, 0][None, :], ln_b[1, 0][None, :],
                     a_feature_major=True)
    xs, _ = _peer_block(xs, xb, ps[1], peer_w_q[1], peer_sub_keys[1], peer_u[1], peer_v[1],
                        ln_g[1, 1], ln_b[1, 1], ple_w_gate[1], ple_w_proj[1])
    return xs.reshape(B, S, D)
```

```python
import functools
import math

import jax
import jax.numpy as jnp
from jax import lax
from jax.experimental import pallas as pl
from jax.experimental.pallas import tpu as pltpu

F32 = jnp.float32
BF16 = jnp.bfloat16

LANES = 128
D_MODEL = 1024
DEPTH = 2
DN_ALPHA = (2.0 * DEPTH) ** 0.25
NORM_EPS = 1e-5

ML_HEADS = 8
ML_QK = 64
ML_V = 128
ML_CHUNK = 256

MLA_HEADS = 8
MLA_NOPE = 128
MLA_ROPE = 64
MLA_V = 128
MLA_KV_RANK = 256
MLA_Q_RANK = 384
MLA_QK_PAD = 256
ROPE_THETA = 10000.0
ATTN_BLOCK = 1024

PEER_HEADS = 8
PEER_KEYS = 128
PEER_HALF = 128
PEER_TOPK = 16
PLE_DIM = 256

ROW_TILE = 512
ROUTE_TILE = 512
ROUTE_GROUP = 256
EXPERT_TOK_TILE = 1024
EXPERT_TILE = 1024
VMEM_LIMIT = 56 * 1024 * 1024

NEG_INF = float("-inf")


def _params(*sem):
    return pltpu.CompilerParams(dimension_semantics=sem, vmem_limit_bytes=VMEM_LIMIT)


def _dot(a, b):
    return jnp.dot(a, b, preferred_element_type=F32)


def _dot_nt(a, b):
    return lax.dot_general(a, b, (((1,), (1,)), ((), ())), preferred_element_type=F32)


def _layer_norm(y, g, b):
    mu = jnp.mean(y, axis=-1, keepdims=True)
    yc = y - mu
    var = jnp.mean(yc * yc, axis=-1, keepdims=True)
    return yc * lax.rsqrt(var + NORM_EPS) * g + b


def _rms_norm(y, g):
    return y * lax.rsqrt(jnp.mean(y * y, axis=-1, keepdims=True) + NORM_EPS) * g


def _full(shape):
    nd = len(shape)
    return pl.BlockSpec(shape, lambda *_: (0,) * nd)


def _mlstm_proj_kernel(x_ref, wqv_ref, wog_ref, wg_ref, wkt_ref, wgt_ref,
                       qv_ref, og_ref, g_ref, kt_ref, gt_ref):
    xb = x_ref[...].astype(BF16)
    qv_ref[...] = _dot(xb, wqv_ref[...]).astype(BF16)
    og_ref[...] = _dot(xb, wog_ref[...])
    g_ref[...] = _dot(xb, wg_ref[...])
    kt_ref[...] = _dot_nt(wkt_ref[...], xb).astype(BF16)
    gt_ref[...] = _dot_nt(wgt_ref[...], xb)


def _mlstm_proj(x, wqv, wog, wg, wkt, wgt):
    T, D = x.shape
    tm = ROW_TILE
    nqv, nk, ng = wqv.shape[1], wkt.shape[0], wgt.shape[0]
    return pl.pallas_call(
        _mlstm_proj_kernel,
        grid=(T // tm,),
        in_specs=[pl.BlockSpec((tm, D), lambda i: (i, 0)),
                  _full(wqv.shape), _full(wog.shape), _full(wg.shape), _full(wkt.shape), _full(wgt.shape)],
        out_specs=[pl.BlockSpec((tm, nqv), lambda i: (i, 0)),
                   pl.BlockSpec((tm, D), lambda i: (i, 0)),
                   pl.BlockSpec((tm, LANES), lambda i: (i, 0)),
                   pl.BlockSpec((nk, tm), lambda i: (0, i)),
                   pl.BlockSpec((ng, tm), lambda i: (0, i))],
        out_shape=[jax.ShapeDtypeStruct((T, nqv), BF16),
                   jax.ShapeDtypeStruct((T, D), F32),
                   jax.ShapeDtypeStruct((T, LANES), F32),
                   jax.ShapeDtypeStruct((nk, T), BF16),
                   jax.ShapeDtypeStruct((ng, T), F32)],
        compiler_params=_params("parallel"),
    )(x, wqv, wog, wg, wkt, wgt)


def _split3(x):
    hi = x.astype(BF16)
    r1 = x - hi.astype(F32)
    mid = r1.astype(BF16)
    lo = (r1 - mid.astype(F32)).astype(BF16)
    return hi, mid, lo


def _log_sigmoid(x):
    return jnp.minimum(x, 0.0) - jnp.log1p(jnp.exp(-jnp.abs(x)))


def _mlstm_scan_kernel(q_ref, v_ref, kt_ref, g_ref, gt_ref, og_ref, bc_ref, br_ref, hng_ref,
                       o_ref, c_ref, m_ref):
    H = ML_HEADS
    L = q_ref.shape[0]

    @pl.when(pl.program_id(0) == 0)
    def _():
        c_ref[...] = jnp.zeros_like(c_ref)
        m_ref[...] = jnp.zeros_like(m_ref)

    row = lax.broadcasted_iota(jnp.int32, (L, L), 0)
    col = lax.broadcasted_iota(jnp.int32, (L, L), 1)
    causal = col <= row
    tril = jnp.where(causal, 1.0, 0.0).astype(BF16)
    triu = jnp.where(row <= col, 1.0, 0.0).astype(BF16)

    g = g_ref[...] + bc_ref[...]
    gt = gt_ref[...] + br_ref[...]
    b_cols = sum(_dot(tril, piece) for piece in _split3(_log_sigmoid(g)))
    b_rows = sum(_dot(piece, triu) for piece in _split3(_log_sigmoid(gt)))

    lane = lax.broadcasted_iota(jnp.int32, (L, LANES), 1)
    ones_col = jnp.where(lane == 0, 1.0, 0.0).astype(BF16)

    for h in range(H):
        b_c = b_cols[:, H + h:H + h + 1]
        b_r = b_rows[H + h:H + h + 1, :]
        ig_r = gt[h:h + 1, :]
        m_prev = m_ref[h:h + 1, 0:1]
        dmat = jnp.where(causal, b_c - b_r + ig_r, NEG_INF)
        log_inter = b_c + m_prev
        m_t = jnp.maximum(log_inter, jnp.max(dmat, axis=1, keepdims=True))
        w = jnp.exp(dmat - m_t)
        s_inter = jnp.exp(log_inter - m_t)

        q = q_ref[:, h * LANES:(h + 1) * LANES]
        kt = kt_ref[h * LANES:(h + 1) * LANES, :]
        v_ext = jnp.concatenate([v_ref[:, h * ML_V:(h + 1) * ML_V], ones_col], axis=1)
        s = (_dot(q, kt) * w).astype(BF16)
        c_prev = c_ref[h]
        nd = _dot(s, v_ext) + s_inter * _dot(q, c_prev.astype(BF16))
        num = nd[:, :ML_V]
        den = nd[:, ML_V:ML_V + 1]
        hh = num / jnp.maximum(jnp.abs(den), jnp.exp(-m_t))

        b_last = b_r[:, L - 1:L]
        log_src = b_last - b_r + ig_r
        m_new = jnp.maximum(b_last + m_prev, jnp.max(log_src, axis=1, keepdims=True))
        w_src = jnp.exp(log_src - m_new)
        s_old = jnp.exp(b_last + m_prev - m_new)
        c_ref[h] = s_old * c_prev + _dot((kt.astype(F32) * w_src).astype(BF16), v_ext)
        m_ref[h:h + 1, :] = jnp.broadcast_to(m_new, (1, LANES))

        mu = jnp.mean(hh, axis=-1, keepdims=True)
        hc = hh - mu
        var = jnp.mean(hc * hc, axis=-1, keepdims=True)
        hn = hc * lax.rsqrt(var + NORM_EPS) * hng_ref[:, h * ML_V:(h + 1) * ML_V]
        og = og_ref[:, h * ML_V:(h + 1) * ML_V]
        o_ref[:, h * ML_V:(h + 1) * ML_V] = (jax.nn.sigmoid(og) * hn).astype(o_ref.dtype)


def _mlstm_scan(qv, kt, g, gt, og, bias_cols, bias_rows, hn_g):
    T = qv.shape[0]
    L = ML_CHUNK
    D = D_MODEL
    return pl.pallas_call(
        _mlstm_scan_kernel,
        grid=(T // L,),
        in_specs=[pl.BlockSpec((L, D), lambda c: (c, 0)),
                  pl.BlockSpec((L, D), lambda c: (c, 1)),
                  pl.BlockSpec((D, L), lambda c: (0, c)),
                  pl.BlockSpec((L, LANES), lambda c: (c, 0)),
                  pl.BlockSpec((2 * ML_HEADS, L), lambda c: (0, c)),
                  pl.BlockSpec((L, D), lambda c: (c, 0)),
                  _full(bias_cols.shape), _full(bias_rows.shape), _full(hn_g.shape)],
        out_specs=pl.BlockSpec((L, D), lambda c: (c, 0)),
        out_shape=jax.ShapeDtypeStruct((T, D), BF16),
        scratch_shapes=[pltpu.VMEM((ML_HEADS, LANES, 2 * LANES), F32),
                        pltpu.VMEM((ML_HEADS, LANES), F32)],
        compiler_params=_params("arbitrary"),
    )(qv, qv, kt, g, gt, og, bias_cols, bias_rows, hn_g)


def _mix_ln_kernel(a_ref, w_ref, x_ref, g_ref, b_ref, o_ref, ob_ref, *, a_feature_major):
    if a_feature_major:
        mix = lax.dot_general(a_ref[...], w_ref[...], (((0,), (0,)), ((), ())), preferred_element_type=F32)
    else:
        mix = _dot(a_ref[...], w_ref[...])
    out = _layer_norm(mix + DN_ALPHA * x_ref[...], g_ref[...], b_ref[...])
    o_ref[...] = out
    ob_ref[...] = out.astype(BF16)


def _mix_ln(a, w, x, g, b, *, a_feature_major=False):
    T, D = x.shape
    tm = ROW_TILE
    if a_feature_major:
        a_spec = pl.BlockSpec((a.shape[0], tm), lambda i: (0, i))
    else:
        a_spec = pl.BlockSpec((tm, a.shape[1]), lambda i: (i, 0))
    return pl.pallas_call(
        functools.partial(_mix_ln_kernel, a_feature_major=a_feature_major),
        grid=(T // tm,),
        in_specs=[a_spec, _full(w.shape),
                  pl.BlockSpec((tm, D), lambda i: (i, 0)), _full(g.shape), _full(b.shape)],
        out_specs=[pl.BlockSpec((tm, D), lambda i: (i, 0))] * 2,
        out_shape=[jax.ShapeDtypeStruct((T, D), F32), jax.ShapeDtypeStruct((T, D), BF16)],
        compiler_params=_params("parallel"),
    )(a, w, x, g, b)


PEER_NTOP = PEER_TOPK + 1
PEER_TOP_ROWS = 24


def _peer_candidate_blocks(a_ref, b_ref):
    blocks = [a_ref[...] + b_ref[0:1, :], b_ref[...] + a_ref[0:1, :]]
    for j in range(1, 8):
        blocks.append(a_ref[0:8, :] + b_ref[j:j + 1, :])
    return jnp.concatenate(blocks, axis=0)


def _bf16_pair_words(x):
    bits = lax.bitcast_convert_type(x, jnp.uint32)
    r = (bits + jnp.uint32(0x7FFF) + ((bits >> 16) & jnp.uint32(1))) >> 16
    return r | (r << 16)


def _peer_route_kernel(x_ref, wq_ref, keys_ref, rk_ref, w2_ref, j1_ref, g1_ref, s_ref, top_ref):
    q = _dot(x_ref[...].astype(BF16), wq_ref[...]).astype(BF16)
    for hc in range(2 * PEER_HEADS):
        s_ref[hc] = _dot_nt(keys_ref[hc], q[:, hc * PEER_HALF:(hc + 1) * PEER_HALF])
    top_ref[...] = jnp.full(top_ref.shape, NEG_INF, F32)
    n_groups = s_ref.shape[2] // ROUTE_GROUP

    def head_group(idx, carry):
        h = idx // n_groups
        ls = pl.ds(pl.multiple_of((idx % n_groups) * ROUTE_GROUP, ROUTE_GROUP), ROUTE_GROUP)
        halves = []
        for c in range(2):
            s = s_ref[2 * h + c, :, ls]
            cur = s
            rank = jnp.full(s.shape, float(PEER_KEYS - 1), F32)
            for r in range(PEER_NTOP):
                mx = jnp.max(cur, axis=0, keepdims=True)
                top_ref[c, r:r + 1, :] = mx
                hit = cur == mx
                if c == 1:
                    rank = jnp.where(hit, float(r), rank)
                cur = jnp.where(hit, NEG_INF, cur)
            halves.append((s, rank))
        (s1, _), (s2, rank2) = halves
        cand = _peer_candidate_blocks(top_ref.at[0], top_ref.at[1])
        a0 = top_ref[0, 0:1, :]
        b0 = top_ref[1, 0:1, :]
        best = a0 + b0
        z = jnp.zeros_like(best)
        t_prev = best
        t_cur = best
        for r in range(PEER_NTOP):
            t_prev = t_cur
            t_cur = jnp.max(cand, axis=0, keepdims=True)
            if r < PEER_TOPK:
                z = z + jnp.exp(t_cur - best)
            cand = jnp.where(cand == t_cur, NEG_INF, cand)
        tau = 0.5 * (t_prev + t_cur)
        thr = tau - s1
        count = jnp.zeros_like(s1)
        for r in range(PEER_NTOP):
            count = jnp.where(top_ref[1, r:r + 1, :] >= thr, float(r + 1), count)
        rk_ref[h, :, ls] = pltpu.bitcast(rank2.astype(BF16), jnp.uint32)
        w2_ref[h, :, ls] = pltpu.bitcast(jnp.exp(s2 - b0).astype(BF16), jnp.uint32)
        j1_ref[h, :, ls] = _bf16_pair_words(count)
        g1_ref[h, :, ls] = _bf16_pair_words(jnp.exp(s1 - a0) / z)
        return carry

    lax.fori_loop(0, PEER_HEADS * n_groups, head_group, 0)


def _peer_route(x, wq, keys):
    T, D = x.shape
    tt = ROUTE_TILE
    ospec = pl.BlockSpec((PEER_HEADS, PEER_KEYS, tt), lambda i: (0, 0, i))
    return pl.pallas_call(
        _peer_route_kernel,
        grid=(T // tt,),
        in_specs=[pl.BlockSpec((tt, D), lambda i: (i, 0)), _full(wq.shape), _full(keys.shape)],
        out_specs=[pl.BlockSpec((PEER_HEADS, PEER_KEYS // 2, tt), lambda i: (0, 0, i))] * 2 + [ospec] * 2,
        out_shape=[jax.ShapeDtypeStruct((PEER_HEADS, PEER_KEYS // 2, T), jnp.uint32)] * 2
        + [jax.ShapeDtypeStruct((PEER_HEADS, PEER_KEYS, T), jnp.uint32)] * 2,
        scratch_shapes=[pltpu.VMEM((2 * PEER_HEADS, PEER_KEYS, tt), F32), pltpu.VMEM((2, PEER_TOP_ROWS, ROUTE_GROUP), F32)],
        compiler_params=_params("parallel"),
    )(x, wq, keys)


def _gelu(x):
    return 0.5 * x * (1.0 + lax.erf(x * (1.0 / math.sqrt(2.0))))


GATE_ROWS = 64


def _row_bf16(words, rows):
    return pltpu.bitcast(jnp.broadcast_to(words, (rows // 2, words.shape[1])), BF16)


def _peer_expert_kernel(x_ref, u_ref, vt_ref, rk_ref, w2_ref, j1_ref, g1_ref, o_ref,
                        acc_ref, act0, act1, p0, p1, *, n_tiles):
    s = pl.program_id(1)
    last = pl.num_programs(1) - 1
    tt = x_ref.shape[0]
    n_sub = u_ref.shape[0] // PEER_KEYS
    n_rb = PEER_KEYS // GATE_ROWS

    @pl.when(s == 0)
    def _():
        acc_ref[...] = jnp.zeros_like(acc_ref)
        p1[...] = jnp.zeros_like(p1)
        act0[...] = _dot_nt(u_ref[...], x_ref[...])

    def gate_block(act_ref, p_ref, lc, rb):
        ls = slice(lc * LANES, (lc + 1) * LANES)
        ws = slice(rb * (GATE_ROWS // 2), (rb + 1) * (GATE_ROWS // 2))
        gates = [None] * n_sub
        for h in range(PEER_HEADS):
            rk = pltpu.bitcast(rk_ref[h, ws, ls], BF16)
            w = pltpu.bitcast(w2_ref[h, ws, ls], BF16)
            for a in range(n_sub):
                count = _row_bf16(j1_ref[h, a:a + 1, ls], GATE_ROWS)
                g = _row_bf16(g1_ref[h, a:a + 1, ls], GATE_ROWS)
                term = jnp.where(rk < count, w, jnp.zeros_like(w)) * g
                gates[a] = term if h == 0 else gates[a] + term
        for a in range(n_sub):
            es = slice(a * PEER_KEYS + rb * GATE_ROWS, a * PEER_KEYS + (rb + 1) * GATE_ROWS)
            p_ref[es, ls] = gates[a] * _gelu(act_ref[es, ls]).astype(BF16)

    def stages(act_w, act_r, p_w, p_r):
        acc_ref[...] += _dot(vt_ref[...], p_r[...])
        act_w[...] = _dot_nt(u_ref[...], x_ref[...])
        for lc in range(tt // LANES):
            for rb in range(n_rb):
                gate_block(act_r, p_w, lc, rb)

    steady = jnp.logical_and(s > 0, s < last)

    @pl.when(jnp.logical_and(steady, s % 2 == 0))
    def _():
        stages(act0, act1, p1, p0)

    @pl.when(jnp.logical_and(steady, s % 2 == 1))
    def _():
        stages(act1, act0, p0, p1)

    @pl.when(s == last)
    def _():
        p_last = p1 if n_tiles % 2 == 0 else p0
        o_ref[...] = (acc_ref[...] + _dot(vt_ref[...], p_last[...])).T


def _peer_expert(xb, u, vt, rk, w2, j1, g1):
    T, D = xb.shape
    E = u.shape[0]
    tt, et = EXPERT_TOK_TILE, EXPERT_TILE
    n_e = E // et
    rspec = pl.BlockSpec((PEER_HEADS, PEER_KEYS // 2, tt), lambda i, s: (0, 0, i))
    sspec = pl.BlockSpec((PEER_HEADS, et // PEER_KEYS, tt),
                         lambda i, s: (0, jnp.clip(s - 1, 0, n_e - 1), i))
    return pl.pallas_call(
        functools.partial(_peer_expert_kernel, n_tiles=n_e),
        grid=(T // tt, n_e + 2),
        in_specs=[pl.BlockSpec((tt, D), lambda i, s: (i, 0)),
                  pl.BlockSpec((et, D), lambda i, s: (jnp.minimum(s, n_e - 1), 0)),
                  pl.BlockSpec((None, D, et), lambda i, s: (jnp.clip(s - 2, 0, n_e - 1), 0, 0)),
                  rspec, rspec, sspec, sspec],
        out_specs=pl.BlockSpec((tt, D), lambda i, s: (i, 0)),
        out_shape=jax.ShapeDtypeStruct((T, D), F32),
        scratch_shapes=[pltpu.VMEM((D, tt), F32), pltpu.VMEM((et, tt), F32), pltpu.VMEM((et, tt), F32),
                        pltpu.VMEM((et, tt), BF16), pltpu.VMEM((et, tt), BF16)],
        compiler_params=_params("parallel", "arbitrary"),
    )(xb, u, vt, rk, w2, j1, g1)


def _ffn_ln_ple_kernel(x_ref, f_ref, p_ref, g_ref, b_ref, wg_ref, wp_ref, o_ref, ob_ref):
    x2 = _layer_norm(DN_ALPHA * x_ref[...] + f_ref[...], g_ref[...], b_ref[...])
    gate = jax.nn.sigmoid(_dot(x2.astype(BF16), wg_ref[...]))
    pe = _dot(p_ref[...].astype(BF16), wp_ref[...])
    out = x2 + gate * pe
    o_ref[...] = out
    ob_ref[...] = out.astype(BF16)


def _ffn_ln_ple(x, f, p, g, b, wg, wp):
    T, D = x.shape
    tm = ROW_TILE
    return pl.pallas_call(
        _ffn_ln_ple_kernel,
        grid=(T // tm,),
        in_specs=[pl.BlockSpec((tm, D), lambda i: (i, 0)), pl.BlockSpec((tm, D), lambda i: (i, 0)),
                  pl.BlockSpec((tm, p.shape[1]), lambda i: (i, 0)),
                  _full(g.shape), _full(b.shape), _full(wg.shape), _full(wp.shape)],
        out_specs=[pl.BlockSpec((tm, D), lambda i: (i, 0))] * 2,
        out_shape=[jax.ShapeDtypeStruct((T, D), F32), jax.ShapeDtypeStruct((T, D), BF16)],
        compiler_params=_params("parallel"),
    )(x, f, p, g, b, wg, wp)


def _mla_kv_kernel(x_ref, wd_ref, g_ref, wkn_ref, wvt_ref, cos_ref, sin_ref, k_ref, vt_ref):
    ckr = _dot(x_ref[...], wd_ref[...])
    ckv = _rms_norm(ckr[:, :MLA_KV_RANK], g_ref[...]).astype(BF16)
    kn = _dot(ckv, wkn_ref[...])
    vt_ref[...] = _dot_nt(wvt_ref[...], ckv).astype(BF16)
    kr = (ckr[:, MLA_KV_RANK:MLA_KV_RANK + LANES] * cos_ref[...]
          + ckr[:, MLA_KV_RANK + LANES:MLA_KV_RANK + 2 * LANES] * sin_ref[...]).astype(BF16)
    for h in range(MLA_HEADS):
        k_ref[:, h * MLA_QK_PAD:h * MLA_QK_PAD + MLA_NOPE] = kn[:, h * MLA_NOPE:(h + 1) * MLA_NOPE].astype(BF16)
        k_ref[:, h * MLA_QK_PAD + MLA_NOPE:(h + 1) * MLA_QK_PAD] = kr


def _mla_kv(xb, wd, g, wkn, wvt, cos, sin):
    T, D = xb.shape
    tm = ROW_TILE
    return pl.pallas_call(
        _mla_kv_kernel,
        grid=(T // tm,),
        in_specs=[pl.BlockSpec((tm, D), lambda i: (i, 0)), _full(wd.shape), _full(g.shape),
                  _full(wkn.shape), _full(wvt.shape),
                  pl.BlockSpec((tm, LANES), lambda i: (i, 0)), pl.BlockSpec((tm, LANES), lambda i: (i, 0))],
        out_specs=[pl.BlockSpec((tm, MLA_HEADS * MLA_QK_PAD), lambda i: (i, 0)),
                   pl.BlockSpec((MLA_HEADS * MLA_V, tm), lambda i: (0, i))],
        out_shape=[jax.ShapeDtypeStruct((T, MLA_HEADS * MLA_QK_PAD), BF16),
                   jax.ShapeDtypeStruct((MLA_HEADS * MLA_V, T), BF16)],
        compiler_params=_params("parallel"),
    )(xb, wd, g, wkn, wvt, cos, sin)


def _mla_q_kernel(x_ref, wdq_ref, g_ref, wuqt_ref, cost_ref, sint_ref, qt_ref):
    scale = (MLA_NOPE + MLA_ROPE) ** -0.5 * math.log2(math.e)
    cq = _rms_norm(_dot(x_ref[...], wdq_ref[...]), g_ref[...]).astype(BF16)
    qa = _dot_nt(wuqt_ref[...], cq)
    rot0 = MLA_HEADS * MLA_QK_PAD
    for h in range(MLA_HEADS):
        base = h * MLA_QK_PAD
        qt_ref[base:base + MLA_NOPE, :] = (qa[base:base + MLA_NOPE, :] * scale).astype(BF16)
        rope = (qa[base + MLA_NOPE:base + MLA_QK_PAD, :] * cost_ref[...]
                + qa[rot0 + h * LANES:rot0 + (h + 1) * LANES, :] * sint_ref[...])
        qt_ref[base + MLA_NOPE:base + MLA_QK_PAD, :] = (rope * scale).astype(BF16)


def _mla_q(xb, wdq, g, wuqt, cost, sint):
    T, D = xb.shape
    tm = ROW_TILE
    return pl.pallas_call(
        _mla_q_kernel,
        grid=(T // tm,),
        in_specs=[pl.BlockSpec((tm, D), lambda i: (i, 0)), _full(wdq.shape), _full(g.shape), _full(wuqt.shape),
                  pl.BlockSpec((LANES, tm), lambda i: (0, i)), pl.BlockSpec((LANES, tm), lambda i: (0, i))],
        out_specs=pl.BlockSpec((MLA_HEADS * MLA_QK_PAD, tm), lambda i: (0, i)),
        out_shape=jax.ShapeDtypeStruct((MLA_HEADS * MLA_QK_PAD, T), BF16),
        compiler_params=_params("parallel"),
    )(xb, wdq, g, wuqt, cost, sint)


def _attn_kernel(qt_ref, k_ref, vt_ref, o_ref, m_ref, l_ref, acc_ref, s_a, s_b):
    i = pl.program_id(1)
    tq = qt_ref.shape[1]
    m_ref[...] = jnp.full(m_ref.shape, NEG_INF, F32)
    l_ref[...] = jnp.zeros_like(l_ref)
    acc_ref[...] = jnp.zeros_like(acc_ref)

    def scores(j, s_ref):
        k = k_ref[pl.ds(pl.multiple_of(j * tq, tq), tq), :]
        s_ref[...] = _dot(k, qt_ref[...])

    def absorb(j, s_ref, masked):
        vt = vt_ref[:, pl.ds(pl.multiple_of(j * tq, tq), tq)]
        s = s_ref[...]
        if masked:
            key = lax.broadcasted_iota(jnp.int32, s.shape, 0)
            qry = lax.broadcasted_iota(jnp.int32, s.shape, 1)
            s = jnp.where(key <= qry, s, NEG_INF)
        m_prev = m_ref[...]
        m_new = jnp.maximum(m_prev, jnp.max(s, axis=0, keepdims=True))
        p = jnp.exp2(s - m_new)
        alpha = jnp.exp2(m_prev - m_new)
        l_ref[...] = alpha * l_ref[...] + jnp.sum(p, axis=0, keepdims=True)
        acc_ref[...] = alpha * acc_ref[...] + _dot(vt, p.astype(BF16))
        m_ref[...] = m_new

    scores(0, s_a)

    def pair(jj, carry):
        scores(2 * jj + 1, s_b)
        absorb(2 * jj, s_a, False)
        scores(2 * jj + 2, s_a)
        absorb(2 * jj + 1, s_b, False)
        return carry

    lax.fori_loop(0, i // 2, pair, 0)

    @pl.when(i % 2 == 0)
    def _():
        absorb(i, s_a, True)

    @pl.when(i % 2 == 1)
    def _():
        scores(i, s_b)
        absorb(i - 1, s_a, False)
        absorb(i, s_b, True)

    o_ref[...] = (acc_ref[...] / l_ref[...]).astype(o_ref.dtype)


def _attention(qt, k, vt):
    T = k.shape[0]
    tq = ATTN_BLOCK
    return pl.pallas_call(
        _attn_kernel,
        grid=(MLA_HEADS, T // tq),
        in_specs=[pl.BlockSpec((MLA_QK_PAD, tq), lambda h, i: (h, i)),
                  pl.BlockSpec((T, MLA_QK_PAD), lambda h, i: (0, h)),
                  pl.BlockSpec((MLA_V, T), lambda h, i: (h, 0))],
        out_specs=pl.BlockSpec((MLA_V, tq), lambda h, i: (h, i)),
        out_shape=jax.ShapeDtypeStruct((MLA_HEADS * MLA_V, T), BF16),
        scratch_shapes=[pltpu.VMEM((1, tq), F32), pltpu.VMEM((1, tq), F32), pltpu.VMEM((MLA_V, tq), F32),
                        pltpu.VMEM((tq, tq), F32), pltpu.VMEM((tq, tq), F32)],
        compiler_params=_params("parallel", "arbitrary"),
    )(qt, k, vt)


def _pad_last(w, width):
    return jnp.pad(w, [(0, 0)] * (w.ndim - 1) + [(0, width - w.shape[-1])])


def _rotate_half_cols(w):
    half = w.shape[-1] // 2
    return jnp.concatenate([-w[..., half:], w[..., :half]], axis=-1)


def _mlstm_weights(w_in, b_if):
    D = D_MODEL
    H = ML_HEADS
    q_end = H * ML_QK
    k_end = 2 * q_end
    v_end = k_end + H * ML_V
    i_end = v_end + H
    f_end = i_end + H
    wq = _pad_last(w_in[:, :q_end].reshape(D, H, ML_QK), LANES).reshape(D, H * LANES)
    wk = _pad_last((w_in[:, q_end:k_end] * (ML_QK ** -0.5)).reshape(D, H, ML_QK), LANES).reshape(D, H * LANES)
    wqv = jnp.concatenate([wq, w_in[:, k_end:v_end]], axis=1).astype(BF16)
    wog = w_in[:, f_end:].astype(BF16)
    wgate = w_in[:, v_end:f_end]
    wg = _pad_last(wgate, LANES).astype(BF16)
    wkt = wk.T.astype(BF16)
    wgt = wgate.T.astype(BF16)
    bias = jnp.concatenate([b_if[0], b_if[1]]).astype(F32)
    bias_cols = _pad_last(bias[None, :], LANES)
    bias_rows = jnp.broadcast_to(bias[:, None], (2 * H, ML_CHUNK))
    return wqv, wog, wg, wkt, wgt, bias_cols, bias_rows


def _mla_weights(kv_w_down, kv_w_up, w_uq):
    H = MLA_HEADS
    wr = kv_w_down[:, MLA_KV_RANK:]
    wd = jnp.concatenate([kv_w_down[:, :MLA_KV_RANK], _pad_last(wr, LANES),
                          _pad_last(_rotate_half_cols(wr), LANES)], axis=1).astype(BF16)
    up = kv_w_up.reshape(MLA_KV_RANK, H, MLA_NOPE + MLA_V)
    wkn = up[:, :, :MLA_NOPE].reshape(MLA_KV_RANK, H * MLA_NOPE).astype(BF16)
    wvt = up[:, :, MLA_NOPE:].reshape(MLA_KV_RANK, H * MLA_V).T.astype(BF16)
    uq = w_uq.reshape(MLA_Q_RANK, H, MLA_NOPE + MLA_ROPE)
    w_a = _pad_last(uq, MLA_QK_PAD).reshape(MLA_Q_RANK, H * MLA_QK_PAD)
    w_r = _pad_last(_rotate_half_cols(uq[:, :, MLA_NOPE:]), LANES).reshape(MLA_Q_RANK, H * LANES)
    wuqt = jnp.concatenate([w_a, w_r], axis=1).T.astype(BF16)
    return wd, wkn, wvt, wuqt


def _rope_tables(positions):
    inv_freq = ROPE_THETA ** (-jnp.arange(0, MLA_ROPE, 2, dtype=F32) / MLA_ROPE)
    ang = positions.astype(F32)[:, None] * inv_freq
    cos = jnp.cos(ang)
    sin = jnp.sin(ang)
    cos = _pad_last(jnp.concatenate([cos, cos], axis=-1), LANES)
    sin = _pad_last(jnp.concatenate([sin, sin], axis=-1), LANES)
    return cos, sin, cos.T, sin.T


def _peer_block(x, xb, p, w_q, sub_keys, u, v, ln_g, ln_b, w_gate, w_proj):
    wq = w_q.astype(BF16)
    keys = sub_keys.reshape(2 * PEER_HEADS, PEER_KEYS, PEER_HALF).astype(BF16)
    rk, w2, j1, g1 = _peer_route(x, wq, keys)
    vt = jnp.swapaxes(v.astype(BF16).reshape(-1, EXPERT_TILE, v.shape[1]), 1, 2)
    ffn = _peer_expert(xb, u.astype(BF16), vt, rk, w2, j1, g1)
    return _ffn_ln_ple(x, ffn, p, ln_g[None, :], ln_b[None, :], w_gate.astype(BF16), w_proj.astype(BF16))


def kernel(x, p, positions, ln_g, ln_b, a_w_in, a_b_if, a_hn_g, a_w_out, kv_w_down, kv_norm_g, kv_w_up,
           b_w_dq, b_q_norm_g, b_w_uq, b_w_out, peer_w_q, peer_sub_keys, peer_u, peer_v, ple_w_proj,
           ple_w_gate):
    B, S, D = x.shape
    xs = x.reshape(B * S, D)
    ps = p.reshape(DEPTH, B * S, PLE_DIM)
    cos, sin, cos_t, sin_t = _rope_tables(positions.reshape(B * S))

    wqv, wog, wg, wkt, wgt, bias_cols, bias_rows = _mlstm_weights(a_w_in[0], a_b_if[0])
    qv, og, g, kt, gt = _mlstm_proj(xs, wqv, wog, wg, wkt, wgt)
    h = _mlstm_scan(qv, kt, g, gt, og, bias_cols, bias_rows, a_hn_g[0][None, :])
    xs, xb = _mix_ln(h, a_w_out[0].astype(BF16), xs, ln_g[0, 0][None, :], ln_b[0, 0][None, :])
    xs, xb = _peer_block(xs, xb, ps[0], peer_w_q[0], peer_sub_keys[0], peer_u[0], peer_v[0],
                         ln_g[0, 1], ln_b[0, 1], ple_w_gate[0], ple_w_proj[0])

    wd, wkn, wvt, wuqt = _mla_weights(kv_w_down, kv_w_up, b_w_uq[0])
    k_all, vt_all = _mla_kv(xb, wd, kv_norm_g[None, :], wkn, wvt, cos, sin)
    qt_all = _mla_q(xb, b_w_dq[0].astype(BF16), b_q_norm_g[0][None, :], wuqt, cos_t, sin_t)
    ot = _attention(qt_all, k_all, vt_all)
    xs, xb = _mix_ln(ot, b_w_out[0].astype(BF16), xs, ln_g[1, 0][None, :], ln_b[1, 0][None, :],
                     a_feature_major=True)
    xs, _ = _peer_block(xs, xb, ps[1], peer_w_q[1], peer_sub_keys[1], peer_u[1], peer_v[1],
                        ln_g[1, 1], ln_b[1, 1], ple_w_gate[1], ple_w_proj[1])
    return xs.reshape(B, S, D)
```

```python
import functools
import math

import jax
import jax.numpy as jnp
from jax import lax
from jax.experimental import pallas as pl
from jax.experimental.pallas import tpu as pltpu

F32 = jnp.float32
BF16 = jnp.bfloat16

LANES = 128
D_MODEL = 1024
DEPTH = 2
DN_ALPHA = (2.0 * DEPTH) ** 0.25
NORM_EPS = 1e-5

ML_HEADS = 8
ML_QK = 64
ML_V = 128
ML_CHUNK = 256

MLA_HEADS = 8
MLA_NOPE = 128
MLA_ROPE = 64
MLA_V = 128
MLA_V_EXT = 144
MLA_KV_RANK = 256
MLA_Q_RANK = 384
MLA_QK_PAD = 256
ROPE_THETA = 10000.0
ATTN_BLOCK = 1024

PEER_HEADS = 8
PEER_KEYS = 128
PEER_HALF = 128
PEER_TOPK = 16
PLE_DIM = 256

ROW_TILE = 512
ROUTE_TILE = 512
ROUTE_GROUP = 256
EXPERT_TOK_TILE = 1024
EXPERT_TILE = 1024
VMEM_LIMIT = 56 * 1024 * 1024

NEG_INF = float("-inf")


def _params(*sem):
    return pltpu.CompilerParams(dimension_semantics=sem, vmem_limit_bytes=VMEM_LIMIT)


def _dot(a, b):
    return jnp.dot(a, b, preferred_element_type=F32)


def _dot_nt(a, b):
    return lax.dot_general(a, b, (((1,), (1,)), ((), ())), preferred_element_type=F32)


def _layer_norm(y, g, b):
    mu = jnp.mean(y, axis=-1, keepdims=True)
    yc = y - mu
    var = jnp.mean(yc * yc, axis=-1, keepdims=True)
    return yc * lax.rsqrt(var + NORM_EPS) * g + b


def _rms_norm(y, g):
    return y * lax.rsqrt(jnp.mean(y * y, axis=-1, keepdims=True) + NORM_EPS) * g


def _full(shape):
    nd = len(shape)
    return pl.BlockSpec(shape, lambda *_: (0,) * nd)


def _mlstm_proj_kernel(x_ref, wqv_ref, wog_ref, wg_ref, wkt_ref, wgt_ref,
                       qv_ref, og_ref, g_ref, kt_ref, gt_ref):
    xb = x_ref[...].astype(BF16)
    qv_ref[...] = _dot(xb, wqv_ref[...]).astype(BF16)
    og_ref[...] = _dot(xb, wog_ref[...])
    g_ref[...] = _dot(xb, wg_ref[...])
    kt_ref[...] = _dot_nt(wkt_ref[...], xb).astype(BF16)
    gt_ref[...] = _dot_nt(wgt_ref[...], xb)


def _mlstm_proj(x, wqv, wog, wg, wkt, wgt):
    T, D = x.shape
    tm = ROW_TILE
    nqv, nk, ng = wqv.shape[1], wkt.shape[0], wgt.shape[0]
    return pl.pallas_call(
        _mlstm_proj_kernel,
        grid=(T // tm,),
        in_specs=[pl.BlockSpec((tm, D), lambda i: (i, 0)),
                  _full(wqv.shape), _full(wog.shape), _full(wg.shape), _full(wkt.shape), _full(wgt.shape)],
        out_specs=[pl.BlockSpec((tm, nqv), lambda i: (i, 0)),
                   pl.BlockSpec((tm, D), lambda i: (i, 0)),
                   pl.BlockSpec((tm, LANES), lambda i: (i, 0)),
                   pl.BlockSpec((nk, tm), lambda i: (0, i)),
                   pl.BlockSpec((ng, tm), lambda i: (0, i))],
        out_shape=[jax.ShapeDtypeStruct((T, nqv), BF16),
                   jax.ShapeDtypeStruct((T, D), F32),
                   jax.ShapeDtypeStruct((T, LANES), F32),
                   jax.ShapeDtypeStruct((nk, T), BF16),
                   jax.ShapeDtypeStruct((ng, T), F32)],
        compiler_params=_params("parallel"),
    )(x, wqv, wog, wg, wkt, wgt)


def _split3(x):
    hi = x.astype(BF16)
    r1 = x - hi.astype(F32)
    mid = r1.astype(BF16)
    lo = (r1 - mid.astype(F32)).astype(BF16)
    return hi, mid, lo


def _log_sigmoid(x):
    return jnp.minimum(x, 0.0) - jnp.log1p(jnp.exp(-jnp.abs(x)))


def _mlstm_scan_kernel(q_ref, v_ref, kt_ref, g_ref, gt_ref, og_ref, bc_ref, br_ref, hng_ref,
                       o_ref, c_ref, m_ref):
    H = ML_HEADS
    L = q_ref.shape[0]

    @pl.when(pl.program_id(0) == 0)
    def _():
        c_ref[...] = jnp.zeros_like(c_ref)
        m_ref[...] = jnp.zeros_like(m_ref)

    row = lax.broadcasted_iota(jnp.int32, (L, L), 0)
    col = lax.broadcasted_iota(jnp.int32, (L, L), 1)
    causal = col <= row
    tril = jnp.where(causal, 1.0, 0.0).astype(BF16)
    triu = jnp.where(row <= col, 1.0, 0.0).astype(BF16)

    g = g_ref[...] + bc_ref[...]
    gt = gt_ref[...] + br_ref[...]
    b_cols = sum(_dot(tril, piece) for piece in _split3(_log_sigmoid(g)))
    b_rows = sum(_dot(piece, triu) for piece in _split3(_log_sigmoid(gt)))

    lane = lax.broadcasted_iota(jnp.int32, (L, LANES), 1)
    ones_col = jnp.where(lane == 0, 1.0, 0.0).astype(BF16)

    for h in range(H):
        b_c = b_cols[:, H + h:H + h + 1]
        b_r = b_rows[H + h:H + h + 1, :]
        ig_r = gt[h:h + 1, :]
        m_prev = m_ref[h:h + 1, 0:1]
        dmat = jnp.where(causal, b_c - b_r + ig_r, NEG_INF)
        log_inter = b_c + m_prev
        m_t = jnp.maximum(log_inter, jnp.max(dmat, axis=1, keepdims=True))
        w = jnp.exp(dmat - m_t)
        s_inter = jnp.exp(log_inter - m_t)

        q = q_ref[:, h * LANES:(h + 1) * LANES]
        kt = kt_ref[h * LANES:(h + 1) * LANES, :]
        v_ext = jnp.concatenate([v_ref[:, h * ML_V:(h + 1) * ML_V], ones_col], axis=1)
        s = (_dot(q, kt) * w).astype(BF16)
        c_prev = c_ref[h]
        nd = _dot(s, v_ext) + s_inter * _dot(q, c_prev.astype(BF16))
        num = nd[:, :ML_V]
        den = nd[:, ML_V:ML_V + 1]
        hh = num / jnp.maximum(jnp.abs(den), jnp.exp(-m_t))

        b_last = b_r[:, L - 1:L]
        log_src = b_last - b_r + ig_r
        m_new = jnp.maximum(b_last + m_prev, jnp.max(log_src, axis=1, keepdims=True))
        w_src = jnp.exp(log_src - m_new)
        s_old = jnp.exp(b_last + m_prev - m_new)
        c_ref[h] = s_old * c_prev + _dot((kt.astype(F32) * w_src).astype(BF16), v_ext)
        m_ref[h:h + 1, :] = jnp.broadcast_to(m_new, (1, LANES))

        mu = jnp.mean(hh, axis=-1, keepdims=True)
        hc = hh - mu
        var = jnp.mean(hc * hc, axis=-1, keepdims=True)
        hn = hc * lax.rsqrt(var + NORM_EPS) * hng_ref[:, h * ML_V:(h + 1) * ML_V]
        og = og_ref[:, h * ML_V:(h + 1) * ML_V]
        o_ref[:, h * ML_V:(h + 1) * ML_V] = (jax.nn.sigmoid(og) * hn).astype(o_ref.dtype)


def _mlstm_scan(qv, kt, g, gt, og, bias_cols, bias_rows, hn_g):
    T = qv.shape[0]
    L = ML_CHUNK
    D = D_MODEL
    return pl.pallas_call(
        _mlstm_scan_kernel,
        grid=(T // L,),
        in_specs=[pl.BlockSpec((L, D), lambda c: (c, 0)),
                  pl.BlockSpec((L, D), lambda c: (c, 1)),
                  pl.BlockSpec((D, L), lambda c: (0, c)),
                  pl.BlockSpec((L, LANES), lambda c: (c, 0)),
                  pl.BlockSpec((2 * ML_HEADS, L), lambda c: (0, c)),
                  pl.BlockSpec((L, D), lambda c: (c, 0)),
                  _full(bias_cols.shape), _full(bias_rows.shape), _full(hn_g.shape)],
        out_specs=pl.BlockSpec((L, D), lambda c: (c, 0)),
        out_shape=jax.ShapeDtypeStruct((T, D), BF16),
        scratch_shapes=[pltpu.VMEM((ML_HEADS, LANES, 2 * LANES), F32),
                        pltpu.VMEM((ML_HEADS, LANES), F32)],
        compiler_params=_params("arbitrary"),
    )(qv, qv, kt, g, gt, og, bias_cols, bias_rows, hn_g)


def _mix_ln_kernel(a_ref, w_ref, x_ref, g_ref, b_ref, o_ref, ob_ref, *, a_feature_major):
    if a_feature_major:
        mix = lax.dot_general(a_ref[...], w_ref[...], (((0,), (0,)), ((), ())), preferred_element_type=F32)
    else:
        mix = _dot(a_ref[...], w_ref[...])
    out = _layer_norm(mix + DN_ALPHA * x_ref[...], g_ref[...], b_ref[...])
    o_ref[...] = out
    ob_ref[...] = out.astype(BF16)


def _mix_ln(a, w, x, g, b, *, a_feature_major=False):
    T, D = x.shape
    tm = ROW_TILE
    if a_feature_major:
        a_spec = pl.BlockSpec((a.shape[0], tm), lambda i: (0, i))
    else:
        a_spec = pl.BlockSpec((tm, a.shape[1]), lambda i: (i, 0))
    return pl.pallas_call(
        functools.partial(_mix_ln_kernel, a_feature_major=a_feature_major),
        grid=(T // tm,),
        in_specs=[a_spec, _full(w.shape),
                  pl.BlockSpec((tm, D), lambda i: (i, 0)), _full(g.shape), _full(b.shape)],
        out_specs=[pl.BlockSpec((tm, D), lambda i: (i, 0))] * 2,
        out_shape=[jax.ShapeDtypeStruct((T, D), F32), jax.ShapeDtypeStruct((T, D), BF16)],
        compiler_params=_params("parallel"),
    )(a, w, x, g, b)


PEER_NTOP = PEER_TOPK + 1
PEER_TOP_ROWS = 24


def _peer_candidate_blocks(a_ref, b_ref):
    blocks = [a_ref[...] + b_ref[0:1, :], b_ref[...] + a_ref[0:1, :]]
    for j in range(1, 8):
        blocks.append(a_ref[0:8, :] + b_ref[j:j + 1, :])
    return jnp.concatenate(blocks, axis=0)


def _bf16_pair_words(x):
    bits = lax.bitcast_convert_type(x, jnp.uint32)
    r = (bits + jnp.uint32(0x7FFF) + ((bits >> 16) & jnp.uint32(1))) >> 16
    return r | (r << 16)


def _peer_route_kernel(x_ref, wq_ref, keys_ref, rk_ref, w2_ref, j1_ref, g1_ref, s_ref, top_ref):
    q = _dot(x_ref[...].astype(BF16), wq_ref[...]).astype(BF16)
    for hc in range(2 * PEER_HEADS):
        s_ref[hc] = _dot_nt(keys_ref[hc], q[:, hc * PEER_HALF:(hc + 1) * PEER_HALF])
    top_ref[...] = jnp.full(top_ref.shape, NEG_INF, F32)
    n_groups = s_ref.shape[2] // ROUTE_GROUP

    def head_group(idx, carry):
        h = idx // n_groups
        ls = pl.ds(pl.multiple_of((idx % n_groups) * ROUTE_GROUP, ROUTE_GROUP), ROUTE_GROUP)
        halves = []
        for c in range(2):
            s = s_ref[2 * h + c, :, ls]
            cur = s
            rank = jnp.full(s.shape, float(PEER_KEYS - 1), F32)
            for r in range(PEER_NTOP):
                mx = jnp.max(cur, axis=0, keepdims=True)
                top_ref[c, r:r + 1, :] = mx
                hit = cur == mx
                if c == 1:
                    rank = jnp.where(hit, float(r), rank)
                cur = jnp.where(hit, NEG_INF, cur)
            halves.append((s, rank))
        (s1, _), (s2, rank2) = halves
        cand = _peer_candidate_blocks(top_ref.at[0], top_ref.at[1])
        a0 = top_ref[0, 0:1, :]
        b0 = top_ref[1, 0:1, :]
        best = a0 + b0
        z = jnp.zeros_like(best)
        t_prev = best
        t_cur = best
        for r in range(PEER_NTOP):
            t_prev = t_cur
            t_cur = jnp.max(cand, axis=0, keepdims=True)
            if r < PEER_TOPK:
                z = z + jnp.exp(t_cur - best)
            cand = jnp.where(cand == t_cur, NEG_INF, cand)
        tau = 0.5 * (t_prev + t_cur)
        thr = tau - s1
        count = jnp.zeros_like(s1)
        for r in range(PEER_NTOP):
            count = jnp.where(top_ref[1, r:r + 1, :] >= thr, float(r + 1), count)
        rk_ref[h, :, ls] = pltpu.bitcast(rank2.astype(BF16), jnp.uint32)
        w2_ref[h, :, ls] = pltpu.bitcast(jnp.exp(s2 - b0).astype(BF16), jnp.uint32)
        j1_ref[h, :, ls] = _bf16_pair_words(count)
        g1_ref[h, :, ls] = _bf16_pair_words(jnp.exp(s1 - a0) / z)
        return carry

    lax.fori_loop(0, PEER_HEADS * n_groups, head_group, 0)


def _peer_route(x, wq, keys):
    T, D = x.shape
    tt = ROUTE_TILE
    ospec = pl.BlockSpec((PEER_HEADS, PEER_KEYS, tt), lambda i: (0, 0, i))
    return pl.pallas_call(
        _peer_route_kernel,
        grid=(T // tt,),
        in_specs=[pl.BlockSpec((tt, D), lambda i: (i, 0)), _full(wq.shape), _full(keys.shape)],
        out_specs=[pl.BlockSpec((PEER_HEADS, PEER_KEYS // 2, tt), lambda i: (0, 0, i))] * 2 + [ospec] * 2,
        out_shape=[jax.ShapeDtypeStruct((PEER_HEADS, PEER_KEYS // 2, T), jnp.uint32)] * 2
        + [jax.ShapeDtypeStruct((PEER_HEADS, PEER_KEYS, T), jnp.uint32)] * 2,
        scratch_shapes=[pltpu.VMEM((2 * PEER_HEADS, PEER_KEYS, tt), F32), pltpu.VMEM((2, PEER_TOP_ROWS, ROUTE_GROUP), F32)],
        compiler_params=_params("parallel"),
    )(x, wq, keys)


def _gelu(x):
    return 0.5 * x * (1.0 + lax.erf(x * (1.0 / math.sqrt(2.0))))


GATE_ROWS = 64
GATE_KEYS = 8


def _row_bf16(words, rows):
    return pltpu.bitcast(jnp.broadcast_to(words, (rows // 2, words.shape[1])), BF16)


def _peer_expert_kernel(x_ref, u_ref, vt_ref, rk_ref, w2_ref, j1_ref, g1_ref, o_ref,
                        acc_ref, act0, act1, p0, p1, *, n_tiles):
    s = pl.program_id(1)
    last = pl.num_programs(1) - 1
    tt = x_ref.shape[0]
    n_sub = u_ref.shape[0] // PEER_KEYS
    n_rb = PEER_KEYS // GATE_ROWS

    @pl.when(s == 0)
    def _():
        acc_ref[...] = jnp.zeros_like(acc_ref)
        p1[...] = jnp.zeros_like(p1)
        act0[...] = _dot_nt(u_ref[...], x_ref[...])

    def gate_block(act_ref, p_ref, lc, rb, kb):
        ls = slice(lc * LANES, (lc + 1) * LANES)
        ws = slice(rb * (GATE_ROWS // 2), (rb + 1) * (GATE_ROWS // 2))
        keys = range(kb * GATE_KEYS, (kb + 1) * GATE_KEYS)
        gates = {}
        for h in range(PEER_HEADS):
            rk = pltpu.bitcast(rk_ref[h, ws, ls], BF16)
            w = pltpu.bitcast(w2_ref[h, ws, ls], BF16)
            for a in keys:
                count = _row_bf16(j1_ref[h, a:a + 1, ls], GATE_ROWS)
                g = _row_bf16(g1_ref[h, a:a + 1, ls], GATE_ROWS)
                term = jnp.where(rk < count, w, jnp.zeros_like(w)) * g
                gates[a] = term if h == 0 else gates[a] + term
        for a in keys:
            es = slice(a * PEER_KEYS + rb * GATE_ROWS, a * PEER_KEYS + (rb + 1) * GATE_ROWS)
            p_ref[es, ls] = gates[a] * _gelu(act_ref[es, ls].astype(BF16))

    def stages(act_w, act_r, p_w, p_r):
        acc_ref[...] += _dot(vt_ref[...], p_r[...])
        act_w[...] = _dot_nt(u_ref[...], x_ref[...])
        for lc in range(tt // LANES):
            for rb in range(n_rb):
                for kb in range(n_sub // GATE_KEYS):
                    gate_block(act_r, p_w, lc, rb, kb)

    steady = jnp.logical_and(s > 0, s < last)

    @pl.when(jnp.logical_and(steady, s % 2 == 0))
    def _():
        stages(act0, act1, p1, p0)

    @pl.when(jnp.logical_and(steady, s % 2 == 1))
    def _():
        stages(act1, act0, p0, p1)

    @pl.when(s == last)
    def _():
        p_last = p1 if n_tiles % 2 == 0 else p0
        o_ref[...] = (acc_ref[...] + _dot(vt_ref[...], p_last[...])).T


def _peer_expert(xb, u, vt, rk, w2, j1, g1):
    T, D = xb.shape
    E = u.shape[0]
    tt, et = EXPERT_TOK_TILE, EXPERT_TILE
    n_e = E // et
    rspec = pl.BlockSpec((PEER_HEADS, PEER_KEYS // 2, tt), lambda i, s: (0, 0, i))
    sspec = pl.BlockSpec((PEER_HEADS, et // PEER_KEYS, tt),
                         lambda i, s: (0, jnp.clip(s - 1, 0, n_e - 1), i))
    return pl.pallas_call(
        functools.partial(_peer_expert_kernel, n_tiles=n_e),
        grid=(T // tt, n_e + 2),
        in_specs=[pl.BlockSpec((tt, D), lambda i, s: (i, 0)),
                  pl.BlockSpec((et, D), lambda i, s: (jnp.minimum(s, n_e - 1), 0)),
                  pl.BlockSpec((None, D, et), lambda i, s: (jnp.clip(s - 2, 0, n_e - 1), 0, 0)),
                  rspec, rspec, sspec, sspec],
        out_specs=pl.BlockSpec((tt, D), lambda i, s: (i, 0)),
        out_shape=jax.ShapeDtypeStruct((T, D), F32),
        scratch_shapes=[pltpu.VMEM((D, tt), F32), pltpu.VMEM((et, tt), F32), pltpu.VMEM((et, tt), F32),
                        pltpu.VMEM((et, tt), BF16), pltpu.VMEM((et, tt), BF16)],
        compiler_params=_params("parallel", "arbitrary"),
    )(xb, u, vt, rk, w2, j1, g1)


def _ffn_ln_ple_kernel(x_ref, f_ref, p_ref, g_ref, b_ref, wg_ref, wp_ref, o_ref, ob_ref):
    x2 = _layer_norm(DN_ALPHA * x_ref[...] + f_ref[...], g_ref[...], b_ref[...])
    gate = jax.nn.sigmoid(_dot(x2.astype(BF16), wg_ref[...]))
    pe = _dot(p_ref[...].astype(BF16), wp_ref[...])
    out = x2 + gate * pe
    o_ref[...] = out
    ob_ref[...] = out.astype(BF16)


def _ffn_ln_ple(x, f, p, g, b, wg, wp):
    T, D = x.shape
    tm = ROW_TILE
    return pl.pallas_call(
        _ffn_ln_ple_kernel,
        grid=(T // tm,),
        in_specs=[pl.BlockSpec((tm, D), lambda i: (i, 0)), pl.BlockSpec((tm, D), lambda i: (i, 0)),
                  pl.BlockSpec((tm, p.shape[1]), lambda i: (i, 0)),
                  _full(g.shape), _full(b.shape), _full(wg.shape), _full(wp.shape)],
        out_specs=[pl.BlockSpec((tm, D), lambda i: (i, 0))] * 2,
        out_shape=[jax.ShapeDtypeStruct((T, D), F32), jax.ShapeDtypeStruct((T, D), BF16)],
        compiler_params=_params("parallel"),
    )(x, f, p, g, b, wg, wp)


def _mla_kv_kernel(x_ref, wd_ref, g_ref, wkn_ref, wvt_ref, cos_ref, sin_ref, k_ref, vt_ref):
    ckr = _dot(x_ref[...], wd_ref[...])
    ckv = _rms_norm(ckr[:, :MLA_KV_RANK], g_ref[...]).astype(BF16)
    kn = _dot(ckv, wkn_ref[...])
    vt = _dot_nt(wvt_ref[...], ckv).astype(BF16)
    pad_row = lax.broadcasted_iota(jnp.int32, (MLA_V_EXT - MLA_V, vt.shape[1]), 0)
    ones_rows = jnp.where(pad_row == 0, 1.0, 0.0).astype(BF16)
    for h in range(MLA_HEADS):
        vt_ref[h * MLA_V_EXT:h * MLA_V_EXT + MLA_V, :] = vt[h * MLA_V:(h + 1) * MLA_V, :]
        vt_ref[h * MLA_V_EXT + MLA_V:(h + 1) * MLA_V_EXT, :] = ones_rows
    kr = (ckr[:, MLA_KV_RANK:MLA_KV_RANK + LANES] * cos_ref[...]
          + ckr[:, MLA_KV_RANK + LANES:MLA_KV_RANK + 2 * LANES] * sin_ref[...]).astype(BF16)
    for h in range(MLA_HEADS):
        k_ref[:, h * MLA_QK_PAD:h * MLA_QK_PAD + MLA_NOPE] = kn[:, h * MLA_NOPE:(h + 1) * MLA_NOPE].astype(BF16)
        k_ref[:, h * MLA_QK_PAD + MLA_NOPE:(h + 1) * MLA_QK_PAD] = kr


def _mla_kv(xb, wd, g, wkn, wvt, cos, sin):
    T, D = xb.shape
    tm = ROW_TILE
    return pl.pallas_call(
        _mla_kv_kernel,
        grid=(T // tm,),
        in_specs=[pl.BlockSpec((tm, D), lambda i: (i, 0)), _full(wd.shape), _full(g.shape),
                  _full(wkn.shape), _full(wvt.shape),
                  pl.BlockSpec((tm, LANES), lambda i: (i, 0)), pl.BlockSpec((tm, LANES), lambda i: (i, 0))],
        out_specs=[pl.BlockSpec((tm, MLA_HEADS * MLA_QK_PAD), lambda i: (i, 0)),
                   pl.BlockSpec((MLA_HEADS * MLA_V_EXT, tm), lambda i: (0, i))],
        out_shape=[jax.ShapeDtypeStruct((T, MLA_HEADS * MLA_QK_PAD), BF16),
                   jax.ShapeDtypeStruct((MLA_HEADS * MLA_V_EXT, T), BF16)],
        compiler_params=_params("parallel"),
    )(xb, wd, g, wkn, wvt, cos, sin)


def _mla_q_kernel(x_ref, wdq_ref, g_ref, wuqt_ref, cost_ref, sint_ref, qt_ref):
    scale = (MLA_NOPE + MLA_ROPE) ** -0.5 * math.log2(math.e)
    cq = _rms_norm(_dot(x_ref[...], wdq_ref[...]), g_ref[...]).astype(BF16)
    qa = _dot_nt(wuqt_ref[...], cq)
    rot0 = MLA_HEADS * MLA_QK_PAD
    for h in range(MLA_HEADS):
        base = h * MLA_QK_PAD
        qt_ref[base:base + MLA_NOPE, :] = (qa[base:base + MLA_NOPE, :] * scale).astype(BF16)
        rope = (qa[base + MLA_NOPE:base + MLA_QK_PAD, :] * cost_ref[...]
                + qa[rot0 + h * LANES:rot0 + (h + 1) * LANES, :] * sint_ref[...])
        qt_ref[base + MLA_NOPE:base + MLA_QK_PAD, :] = (rope * scale).astype(BF16)


def _mla_q(xb, wdq, g, wuqt, cost, sint):
    T, D = xb.shape
    tm = ROW_TILE
    return pl.pallas_call(
        _mla_q_kernel,
        grid=(T // tm,),
        in_specs=[pl.BlockSpec((tm, D), lambda i: (i, 0)), _full(wdq.shape), _full(g.shape), _full(wuqt.shape),
                  pl.BlockSpec((LANES, tm), lambda i: (0, i)), pl.BlockSpec((LANES, tm), lambda i: (0, i))],
        out_specs=pl.BlockSpec((MLA_HEADS * MLA_QK_PAD, tm), lambda i: (0, i)),
        out_shape=jax.ShapeDtypeStruct((MLA_HEADS * MLA_QK_PAD, T), BF16),
        compiler_params=_params("parallel"),
    )(xb, wdq, g, wuqt, cost, sint)


def _attn_kernel(qt_ref, k_ref, vt_ref, o_ref, m_ref, acc_ref, s_a, s_b):
    i = pl.program_id(1)
    tq = qt_ref.shape[1]
    m_ref[...] = jnp.full(m_ref.shape, NEG_INF, F32)
    acc_ref[...] = jnp.zeros_like(acc_ref)

    def scores(j, s_ref):
        k = k_ref[pl.ds(pl.multiple_of(j * tq, tq), tq), :]
        s_ref[...] = _dot(k, qt_ref[...])

    def absorb(j, s_ref, masked):
        vt = vt_ref[:, pl.ds(pl.multiple_of(j * tq, tq), tq)]
        s = s_ref[...]
        if masked:
            key = lax.broadcasted_iota(jnp.int32, s.shape, 0)
            qry = lax.broadcasted_iota(jnp.int32, s.shape, 1)
            s = jnp.where(key <= qry, s, NEG_INF)
        m_prev = m_ref[...]
        m_new = jnp.maximum(m_prev, jnp.max(s, axis=0, keepdims=True))
        p = jnp.exp2(s - m_new)
        alpha = jnp.exp2(m_prev - m_new)
        acc_ref[...] = alpha * acc_ref[...] + _dot(vt, p.astype(BF16))
        m_ref[...] = m_new

    scores(0, s_a)

    def pair(jj, carry):
        scores(2 * jj + 1, s_b)
        absorb(2 * jj, s_a, False)
        scores(2 * jj + 2, s_a)
        absorb(2 * jj + 1, s_b, False)
        return carry

    lax.fori_loop(0, i // 2, pair, 0)

    @pl.when(i % 2 == 0)
    def _():
        absorb(i, s_a, True)

    @pl.when(i % 2 == 1)
    def _():
        scores(i, s_b)
        absorb(i - 1, s_a, False)
        absorb(i, s_b, True)

    o_ref[...] = (acc_ref[:MLA_V, :] / acc_ref[MLA_V:MLA_V + 1, :]).astype(o_ref.dtype)


def _attention(qt, k, vt):
    T = k.shape[0]
    tq = ATTN_BLOCK
    return pl.pallas_call(
        _attn_kernel,
        grid=(MLA_HEADS, T // tq),
        in_specs=[pl.BlockSpec((MLA_QK_PAD, tq), lambda h, i: (h, i)),
                  pl.BlockSpec((T, MLA_QK_PAD), lambda h, i: (0, h)),
                  pl.BlockSpec((MLA_V_EXT, T), lambda h, i: (h, 0))],
        out_specs=pl.BlockSpec((MLA_V, tq), lambda h, i: (h, i)),
        out_shape=jax.ShapeDtypeStruct((MLA_HEADS * MLA_V, T), BF16),
        scratch_shapes=[pltpu.VMEM((1, tq), F32), pltpu.VMEM((MLA_V_EXT, tq), F32),
                        pltpu.VMEM((tq, tq), F32), pltpu.VMEM((tq, tq), F32)],
        compiler_params=_params("parallel", "arbitrary"),
    )(qt, k, vt)


def _pad_last(w, width):
    return jnp.pad(w, [(0, 0)] * (w.ndim - 1) + [(0, width - w.shape[-1])])


def _rotate_half_cols(w):
    half = w.shape[-1] // 2
    return jnp.concatenate([-w[..., half:], w[..., :half]], axis=-1)


def _mlstm_weights(w_in, b_if):
    D = D_MODEL
    H = ML_HEADS
    q_end = H * ML_QK
    k_end = 2 * q_end
    v_end = k_end + H * ML_V
    i_end = v_end + H
    f_end = i_end + H
    wq = _pad_last(w_in[:, :q_end].reshape(D, H, ML_QK), LANES).reshape(D, H * LANES)
    wk = _pad_last((w_in[:, q_end:k_end] * (ML_QK ** -0.5)).reshape(D, H, ML_QK), LANES).reshape(D, H * LANES)
    wqv = jnp.concatenate([wq, w_in[:, k_end:v_end]], axis=1).astype(BF16)
    wog = w_in[:, f_end:].astype(BF16)
    wgate = w_in[:, v_end:f_end]
    wg = _pad_last(wgate, LANES).astype(BF16)
    wkt = wk.T.astype(BF16)
    wgt = wgate.T.astype(BF16)
    bias = jnp.concatenate([b_if[0], b_if[1]]).astype(F32)
    bias_cols = _pad_last(bias[None, :], LANES)
    bias_rows = jnp.broadcast_to(bias[:, None], (2 * H, ML_CHUNK))
    return wqv, wog, wg, wkt, wgt, bias_cols, bias_rows


def _mla_weights(kv_w_down, kv_w_up, w_uq):
    H = MLA_HEADS
    wr = kv_w_down[:, MLA_KV_RANK:]
    wd = jnp.concatenate([kv_w_down[:, :MLA_KV_RANK], _pad_last(wr, LANES),
                          _pad_last(_rotate_half_cols(wr), LANES)], axis=1).astype(BF16)
    up = kv_w_up.reshape(MLA_KV_RANK, H, MLA_NOPE + MLA_V)
    wkn = up[:, :, :MLA_NOPE].reshape(MLA_KV_RANK, H * MLA_NOPE).astype(BF16)
    wvt = up[:, :, MLA_NOPE:].reshape(MLA_KV_RANK, H * MLA_V).T.astype(BF16)
    uq = w_uq.reshape(MLA_Q_RANK, H, MLA_NOPE + MLA_ROPE)
    w_a = _pad_last(uq, MLA_QK_PAD).reshape(MLA_Q_RANK, H * MLA_QK_PAD)
    w_r = _pad_last(_rotate_half_cols(uq[:, :, MLA_NOPE:]), LANES).reshape(MLA_Q_RANK, H * LANES)
    wuqt = jnp.concatenate([w_a, w_r], axis=1).T.astype(BF16)
    return wd, wkn, wvt, wuqt


def _rope_tables(positions):
    inv_freq = ROPE_THETA ** (-jnp.arange(0, MLA_ROPE, 2, dtype=F32) / MLA_ROPE)
    ang = positions.astype(F32)[:, None] * inv_freq
    cos = jnp.cos(ang)
    sin = jnp.sin(ang)
    cos = _pad_last(jnp.concatenate([cos, cos], axis=-1), LANES)
    sin = _pad_last(jnp.concatenate([sin, sin], axis=-1), LANES)
    return cos, sin, cos.T, sin.T


def _peer_block(x, xb, p, w_q, sub_keys, u, v, ln_g, ln_b, w_gate, w_proj):
    wq = w_q.astype(BF16)
    keys = sub_keys.reshape(2 * PEER_HEADS, PEER_KEYS, PEER_HALF).astype(BF16)
    rk, w2, j1, g1 = _peer_route(x, wq, keys)
    vt = jnp.swapaxes(v.astype(BF16).reshape(-1, EXPERT_TILE, v.shape[1]), 1, 2)
    ffn = _peer_expert(xb, u.astype(BF16), vt, rk, w2, j1, g1)
    return _ffn_ln_ple(x, ffn, p, ln_g[None, :], ln_b[None, :], w_gate.astype(BF16), w_proj.astype(BF16))


def kernel(x, p, positions, ln_g, ln_b, a_w_in, a_b_if, a_hn_g, a_w_out, kv_w_down, kv_norm_g, kv_w_up,
           b_w_dq, b_q_norm_g, b_w_uq, b_w_out, peer_w_q, peer_sub_keys, peer_u, peer_v, ple_w_proj,
           ple_w_gate):
    B, S, D = x.shape
    xs = x.reshape(B * S, D)
    ps = p.reshape(DEPTH, B * S, PLE_DIM)
    cos, sin, cos_t, sin_t = _rope_tables(positions.reshape(B * S))

    wqv, wog, wg, wkt, wgt, bias_cols, bias_rows = _mlstm_weights(a_w_in[0], a_b_if[0])
    qv, og, g, kt, gt = _mlstm_proj(xs, wqv, wog, wg, wkt, wgt)
    h = _mlstm_scan(qv, kt, g, gt, og, bias_cols, bias_rows, a_hn_g[0][None, :])
    xs, xb = _mix_ln(h, a_w_out[0].astype(BF16), xs, ln_g[0, 0][None, :], ln_b[0, 0][None, :])
    xs, xb = _peer_block(xs, xb, ps[0], peer_w_q[0], peer_sub_keys[0], peer_u[0], peer_v[0],
                         ln_g[0, 1], ln_b[0, 1], ple_w_gate[0], ple_w_proj[0])

    wd, wkn, wvt, wuqt = _mla_weights(kv_w_down, kv_w_up, b_w_uq[0])
    k_all, vt_all = _mla_kv(xb, wd, kv_norm_g[None, :], wkn, wvt, cos, sin)
    qt_all = _mla_q(xb, b_w_dq[0].astype(BF16), b_q_norm_g[0][None, :], wuqt, cos_t, sin_t)
    ot = _attention(qt_all, k_all, vt_all)
    xs, xb = _mix_ln(ot, b_w_out[0].astype(BF16), xs, ln_g[1, 0][None, :], ln_b[1, 0][None, :],
                     a_feature_major=True)
    xs, _ = _peer_block(xs, xb, ps[1], peer_w_q[1], peer_sub_keys[1], peer_u[1], peer_v[1],
                        ln_g[1, 1], ln_b[1, 1], ple_w_gate[1], ple_w_proj[1])
    return xs.reshape(B, S, D)
```

```python
import functools
import math

import jax
import jax.numpy as jnp
from jax import lax
from jax.experimental import pallas as pl
from jax.experimental.pallas import tpu as pltpu

F32 = jnp.float32
BF16 = jnp.bfloat16

LANES = 128
D_MODEL = 1024
DEPTH = 2
DN_ALPHA = (2.0 * DEPTH) ** 0.25
NORM_EPS = 1e-5

ML_HEADS = 8
ML_QK = 64
ML_V = 128
ML_CHUNK = 256

MLA_HEADS = 8
MLA_NOPE = 128
MLA_ROPE = 64
MLA_V = 128
MLA_V_EXT = 144
MLA_KV_RANK = 256
MLA_Q_RANK = 384
MLA_QK_PAD = 256
ROPE_THETA = 10000.0
ATTN_BLOCK = 1024

PEER_HEADS = 8
PEER_KEYS = 128
PEER_HALF = 128
PEER_TOPK = 16
PLE_DIM = 256

ROW_TILE = 512
ROUTE_TILE = 512
ROUTE_GROUP = 256
EXPERT_TOK_TILE = 1024
EXPERT_TILE = 1024
VMEM_LIMIT = 56 * 1024 * 1024

NEG_INF = float("-inf")


def _params(*sem):
    return pltpu.CompilerParams(dimension_semantics=sem, vmem_limit_bytes=VMEM_LIMIT)


def _dot(a, b):
    return jnp.dot(a, b, preferred_element_type=F32)


def _dot_nt(a, b):
    return lax.dot_general(a, b, (((1,), (1,)), ((), ())), preferred_element_type=F32)


def _layer_norm(y, g, b):
    mu = jnp.mean(y, axis=-1, keepdims=True)
    yc = y - mu
    var = jnp.mean(yc * yc, axis=-1, keepdims=True)
    return yc * lax.rsqrt(var + NORM_EPS) * g + b


def _rms_norm(y, g):
    return y * lax.rsqrt(jnp.mean(y * y, axis=-1, keepdims=True) + NORM_EPS) * g


def _full(shape):
    nd = len(shape)
    return pl.BlockSpec(shape, lambda *_: (0,) * nd)


def _mlstm_proj_kernel(x_ref, wqv_ref, wog_ref, wg_ref, wkt_ref, wgt_ref,
                       qv_ref, og_ref, g_ref, kt_ref, gt_ref):
    xb = x_ref[...].astype(BF16)
    qv_ref[...] = _dot(xb, wqv_ref[...]).astype(BF16)
    og_ref[...] = _dot(xb, wog_ref[...])
    g_ref[...] = _dot(xb, wg_ref[...])
    kt_ref[...] = _dot_nt(wkt_ref[...], xb).astype(BF16)
    gt_ref[...] = _dot_nt(wgt_ref[...], xb)


def _mlstm_proj(x, wqv, wog, wg, wkt, wgt):
    T, D = x.shape
    tm = ROW_TILE
    nqv, nk, ng = wqv.shape[1], wkt.shape[0], wgt.shape[0]
    return pl.pallas_call(
        _mlstm_proj_kernel,
        grid=(T // tm,),
        in_specs=[pl.BlockSpec((tm, D), lambda i: (i, 0)),
                  _full(wqv.shape), _full(wog.shape), _full(wg.shape), _full(wkt.shape), _full(wgt.shape)],
        out_specs=[pl.BlockSpec((tm, nqv), lambda i: (i, 0)),
                   pl.BlockSpec((tm, D), lambda i: (i, 0)),
                   pl.BlockSpec((tm, LANES), lambda i: (i, 0)),
                   pl.BlockSpec((nk, tm), lambda i: (0, i)),
                   pl.BlockSpec((ng, tm), lambda i: (0, i))],
        out_shape=[jax.ShapeDtypeStruct((T, nqv), BF16),
                   jax.ShapeDtypeStruct((T, D), F32),
                   jax.ShapeDtypeStruct((T, LANES), F32),
                   jax.ShapeDtypeStruct((nk, T), BF16),
                   jax.ShapeDtypeStruct((ng, T), F32)],
        compiler_params=_params("parallel"),
    )(x, wqv, wog, wg, wkt, wgt)


def _split3(x):
    hi = x.astype(BF16)
    r1 = x - hi.astype(F32)
    mid = r1.astype(BF16)
    lo = (r1 - mid.astype(F32)).astype(BF16)
    return hi, mid, lo


def _log_sigmoid(x):
    return jnp.minimum(x, 0.0) - jnp.log1p(jnp.exp(-jnp.abs(x)))


def _mlstm_scan_kernel(q_ref, v_ref, kt_ref, g_ref, gt_ref, og_ref, bc_ref, br_ref, hng_ref,
                       o_ref, c_ref, m_ref):
    H = ML_HEADS
    L = q_ref.shape[0]

    @pl.when(pl.program_id(0) == 0)
    def _():
        c_ref[...] = jnp.zeros_like(c_ref)
        m_ref[...] = jnp.zeros_like(m_ref)

    row = lax.broadcasted_iota(jnp.int32, (L, L), 0)
    col = lax.broadcasted_iota(jnp.int32, (L, L), 1)
    causal = col <= row
    tril = jnp.where(causal, 1.0, 0.0).astype(BF16)
    triu = jnp.where(row <= col, 1.0, 0.0).astype(BF16)

    g = g_ref[...] + bc_ref[...]
    gt = gt_ref[...] + br_ref[...]
    b_cols = sum(_dot(tril, piece) for piece in _split3(_log_sigmoid(g)))
    b_rows = sum(_dot(piece, triu) for piece in _split3(_log_sigmoid(gt)))

    lane = lax.broadcasted_iota(jnp.int32, (L, LANES), 1)
    ones_col = jnp.where(lane == 0, 1.0, 0.0).astype(BF16)

    for h in range(H):
        b_c = b_cols[:, H + h:H + h + 1]
        b_r = b_rows[H + h:H + h + 1, :]
        ig_r = gt[h:h + 1, :]
        m_prev = m_ref[h:h + 1, 0:1]
        dmat = jnp.where(causal, b_c - b_r + ig_r, NEG_INF)
        log_inter = b_c + m_prev
        m_t = jnp.maximum(log_inter, jnp.max(dmat, axis=1, keepdims=True))
        w = jnp.exp(dmat - m_t)
        s_inter = jnp.exp(log_inter - m_t)

        q = q_ref[:, h * LANES:(h + 1) * LANES]
        kt = kt_ref[h * LANES:(h + 1) * LANES, :]
        v_ext = jnp.concatenate([v_ref[:, h * ML_V:(h + 1) * ML_V], ones_col], axis=1)
        s = (_dot(q, kt) * w).astype(BF16)
        c_prev = c_ref[h]
        nd = _dot(s, v_ext) + s_inter * _dot(q, c_prev.astype(BF16))
        num = nd[:, :ML_V]
        den = nd[:, ML_V:ML_V + 1]
        hh = num / jnp.maximum(jnp.abs(den), jnp.exp(-m_t))

        b_last = b_r[:, L - 1:L]
        log_src = b_last - b_r + ig_r
        m_new = jnp.maximum(b_last + m_prev, jnp.max(log_src, axis=1, keepdims=True))
        w_src = jnp.exp(log_src - m_new)
        s_old = jnp.exp(b_last + m_prev - m_new)
        c_ref[h] = s_old * c_prev + _dot((kt.astype(F32) * w_src).astype(BF16), v_ext)
        m_ref[h:h + 1, :] = jnp.broadcast_to(m_new, (1, LANES))

        mu = jnp.mean(hh, axis=-1, keepdims=True)
        hc = hh - mu
        var = jnp.mean(hc * hc, axis=-1, keepdims=True)
        hn = hc * lax.rsqrt(var + NORM_EPS) * hng_ref[:, h * ML_V:(h + 1) * ML_V]
        og = og_ref[:, h * ML_V:(h + 1) * ML_V]
        o_ref[:, h * ML_V:(h + 1) * ML_V] = (jax.nn.sigmoid(og) * hn).astype(o_ref.dtype)


def _mlstm_scan(qv, kt, g, gt, og, bias_cols, bias_rows, hn_g):
    T = qv.shape[0]
    L = ML_CHUNK
    D = D_MODEL
    return pl.pallas_call(
        _mlstm_scan_kernel,
        grid=(T // L,),
        in_specs=[pl.BlockSpec((L, D), lambda c: (c, 0)),
                  pl.BlockSpec((L, D), lambda c: (c, 1)),
                  pl.BlockSpec((D, L), lambda c: (0, c)),
                  pl.BlockSpec((L, LANES), lambda c: (c, 0)),
                  pl.BlockSpec((2 * ML_HEADS, L), lambda c: (0, c)),
                  pl.BlockSpec((L, D), lambda c: (c, 0)),
                  _full(bias_cols.shape), _full(bias_rows.shape), _full(hn_g.shape)],
        out_specs=pl.BlockSpec((L, D), lambda c: (c, 0)),
        out_shape=jax.ShapeDtypeStruct((T, D), BF16),
        scratch_shapes=[pltpu.VMEM((ML_HEADS, LANES, 2 * LANES), F32),
                        pltpu.VMEM((ML_HEADS, LANES), F32)],
        compiler_params=_params("arbitrary"),
    )(qv, qv, kt, g, gt, og, bias_cols, bias_rows, hn_g)


def _mix_ln_kernel(a_ref, w_ref, x_ref, g_ref, b_ref, o_ref, ob_ref, *, a_feature_major):
    if a_feature_major:
        mix = lax.dot_general(a_ref[...], w_ref[...], (((0,), (0,)), ((), ())), preferred_element_type=F32)
    else:
        mix = _dot(a_ref[...], w_ref[...])
    out = _layer_norm(mix + DN_ALPHA * x_ref[...], g_ref[...], b_ref[...])
    o_ref[...] = out
    ob_ref[...] = out.astype(BF16)


def _mix_ln(a, w, x, g, b, *, a_feature_major=False):
    T, D = x.shape
    tm = ROW_TILE
    if a_feature_major:
        a_spec = pl.BlockSpec((a.shape[0], tm), lambda i: (0, i))
    else:
        a_spec = pl.BlockSpec((tm, a.shape[1]), lambda i: (i, 0))
    return pl.pallas_call(
        functools.partial(_mix_ln_kernel, a_feature_major=a_feature_major),
        grid=(T // tm,),
        in_specs=[a_spec, _full(w.shape),
                  pl.BlockSpec((tm, D), lambda i: (i, 0)), _full(g.shape), _full(b.shape)],
        out_specs=[pl.BlockSpec((tm, D), lambda i: (i, 0))] * 2,
        out_shape=[jax.ShapeDtypeStruct((T, D), F32), jax.ShapeDtypeStruct((T, D), BF16)],
        compiler_params=_params("parallel"),
    )(a, w, x, g, b)


PEER_NTOP = PEER_TOPK + 1
PEER_TOP_ROWS = 24


def _peer_candidate_blocks(a_ref, b_ref):
    blocks = [a_ref[...] + b_ref[0:1, :], b_ref[...] + a_ref[0:1, :]]
    for j in range(1, 8):
        blocks.append(a_ref[0:8, :] + b_ref[j:j + 1, :])
    return jnp.concatenate(blocks, axis=0)


def _bf16_pair_words(x):
    bits = lax.bitcast_convert_type(x, jnp.uint32)
    r = (bits + jnp.uint32(0x7FFF) + ((bits >> 16) & jnp.uint32(1))) >> 16
    return r | (r << 16)


def _peer_route_kernel(x_ref, wq_ref, keys_ref, rk_ref, w2_ref, j1_ref, g1_ref, s_ref, top_ref):
    q = _dot(x_ref[...].astype(BF16), wq_ref[...]).astype(BF16)
    for hc in range(2 * PEER_HEADS):
        s_ref[hc] = _dot_nt(keys_ref[hc], q[:, hc * PEER_HALF:(hc + 1) * PEER_HALF])
    top_ref[...] = jnp.full(top_ref.shape, NEG_INF, F32)
    n_groups = s_ref.shape[2] // ROUTE_GROUP

    def head_group(idx, carry):
        h = idx // n_groups
        ls = pl.ds(pl.multiple_of((idx % n_groups) * ROUTE_GROUP, ROUTE_GROUP), ROUTE_GROUP)
        halves = []
        for c in range(2):
            s = s_ref[2 * h + c, :, ls]
            cur = s
            rank = jnp.full(s.shape, float(PEER_KEYS - 1), F32)
            for r in range(PEER_NTOP):
                mx = jnp.max(cur, axis=0, keepdims=True)
                top_ref[c, r:r + 1, :] = mx
                hit = cur == mx
                if c == 1:
                    rank = jnp.where(hit, float(r), rank)
                cur = jnp.where(hit, NEG_INF, cur)
            halves.append((s, rank))
        (s1, _), (s2, rank2) = halves
        cand = _peer_candidate_blocks(top_ref.at[0], top_ref.at[1])
        a0 = top_ref[0, 0:1, :]
        b0 = top_ref[1, 0:1, :]
        best = a0 + b0
        z = jnp.zeros_like(best)
        t_prev = best
        t_cur = best
        for r in range(PEER_NTOP):
            t_prev = t_cur
            t_cur = jnp.max(cand, axis=0, keepdims=True)
            if r < PEER_TOPK:
                z = z + jnp.exp(t_cur - best)
            cand = jnp.where(cand == t_cur, NEG_INF, cand)
        tau = 0.5 * (t_prev + t_cur)
        thr = tau - s1
        count = jnp.zeros_like(s1)
        for r in range(PEER_NTOP):
            count = jnp.where(top_ref[1, r:r + 1, :] >= thr, float(r + 1), count)
        rk_ref[h, :, ls] = pltpu.bitcast(rank2.astype(BF16), jnp.uint32)
        w2_ref[h, :, ls] = pltpu.bitcast(jnp.exp(s2 - b0).astype(BF16), jnp.uint32)
        j1_ref[h, :, ls] = _bf16_pair_words(count)
        g1_ref[h, :, ls] = _bf16_pair_words(jnp.exp(s1 - a0) / z)
        return carry

    lax.fori_loop(0, PEER_HEADS * n_groups, head_group, 0)


def _peer_route(x, wq, keys):
    T, D = x.shape
    tt = ROUTE_TILE
    ospec = pl.BlockSpec((PEER_HEADS, PEER_KEYS, tt), lambda i: (0, 0, i))
    return pl.pallas_call(
        _peer_route_kernel,
        grid=(T // tt,),
        in_specs=[pl.BlockSpec((tt, D), lambda i: (i, 0)), _full(wq.shape), _full(keys.shape)],
        out_specs=[pl.BlockSpec((PEER_HEADS, PEER_KEYS // 2, tt), lambda i: (0, 0, i))] * 2 + [ospec] * 2,
        out_shape=[jax.ShapeDtypeStruct((PEER_HEADS, PEER_KEYS // 2, T), jnp.uint32)] * 2
        + [jax.ShapeDtypeStruct((PEER_HEADS, PEER_KEYS, T), jnp.uint32)] * 2,
        scratch_shapes=[pltpu.VMEM((2 * PEER_HEADS, PEER_KEYS, tt), F32), pltpu.VMEM((2, PEER_TOP_ROWS, ROUTE_GROUP), F32)],
        compiler_params=_params("parallel"),
    )(x, wq, keys)


def _gelu(x):
    return 0.5 * x * (1.0 + lax.erf(x * (1.0 / math.sqrt(2.0))))


GATE_ROWS = 64
GATE_KEYS = 8


def _row_bf16(words, rows):
    return pltpu.bitcast(jnp.broadcast_to(words, (rows // 2, words.shape[1])), BF16)


def _peer_expert_kernel(x_ref, u_ref, vt_ref, rk_ref, w2_ref, j1_ref, g1_ref, o_ref,
                        acc_ref, act0, act1, p0, p1, *, n_tiles):
    s = pl.program_id(1)
    last = pl.num_programs(1) - 1
    tt = x_ref.shape[0]
    n_sub = u_ref.shape[0] // PEER_KEYS
    n_rb = PEER_KEYS // GATE_ROWS

    @pl.when(s == 0)
    def _():
        acc_ref[...] = jnp.zeros_like(acc_ref)
        p1[...] = jnp.zeros_like(p1)
        act0[...] = _dot_nt(u_ref[...], x_ref[...]).astype(BF16)

    def gate_block(act_ref, p_ref, lc, rb, kb):
        ls = slice(lc * LANES, (lc + 1) * LANES)
        ws = slice(rb * (GATE_ROWS // 2), (rb + 1) * (GATE_ROWS // 2))
        keys = range(kb * GATE_KEYS, (kb + 1) * GATE_KEYS)
        gates = {}
        for h in range(PEER_HEADS):
            rk = pltpu.bitcast(rk_ref[h, ws, ls], BF16)
            w = pltpu.bitcast(w2_ref[h, ws, ls], BF16)
            for a in keys:
                count = _row_bf16(j1_ref[h, a:a + 1, ls], GATE_ROWS)
                g = _row_bf16(g1_ref[h, a:a + 1, ls], GATE_ROWS)
                term = jnp.where(rk < count, w, jnp.zeros_like(w)) * g
                gates[a] = term if h == 0 else gates[a] + term
        for a in keys:
            es = slice(a * PEER_KEYS + rb * GATE_ROWS, a * PEER_KEYS + (rb + 1) * GATE_ROWS)
            p_ref[es, ls] = gates[a] * _gelu(act_ref[es, ls])

    def stages(act_w, act_r, p_w, p_r):
        acc_ref[...] += _dot(vt_ref[...], p_r[...])
        act_w[...] = _dot_nt(u_ref[...], x_ref[...]).astype(BF16)
        for lc in range(tt // LANES):
            for rb in range(n_rb):
                for kb in range(n_sub // GATE_KEYS):
                    gate_block(act_r, p_w, lc, rb, kb)

    steady = jnp.logical_and(s > 0, s < last)

    @pl.when(jnp.logical_and(steady, s % 2 == 0))
    def _():
        stages(act0, act1, p1, p0)

    @pl.when(jnp.logical_and(steady, s % 2 == 1))
    def _():
        stages(act1, act0, p0, p1)

    @pl.when(s == last)
    def _():
        p_last = p1 if n_tiles % 2 == 0 else p0
        o_ref[...] = (acc_ref[...] + _dot(vt_ref[...], p_last[...])).T.astype(o_ref.dtype)


def _peer_expert(xb, u, vt, rk, w2, j1, g1):
    T, D = xb.shape
    E = u.shape[0]
    tt, et = EXPERT_TOK_TILE, EXPERT_TILE
    n_e = E // et
    rspec = pl.BlockSpec((PEER_HEADS, PEER_KEYS // 2, tt), lambda i, s: (0, 0, i))
    sspec = pl.BlockSpec((PEER_HEADS, et // PEER_KEYS, tt),
                         lambda i, s: (0, jnp.clip(s - 1, 0, n_e - 1), i))
    return pl.pallas_call(
        functools.partial(_peer_expert_kernel, n_tiles=n_e),
        grid=(T // tt, n_e + 2),
        in_specs=[pl.BlockSpec((tt, D), lambda i, s: (i, 0)),
                  pl.BlockSpec((et, D), lambda i, s: (jnp.minimum(s, n_e - 1), 0)),
                  pl.BlockSpec((None, D, et), lambda i, s: (jnp.clip(s - 2, 0, n_e - 1), 0, 0)),
                  rspec, rspec, sspec, sspec],
        out_specs=pl.BlockSpec((tt, D), lambda i, s: (i, 0)),
        out_shape=jax.ShapeDtypeStruct((T, D), BF16),
        scratch_shapes=[pltpu.VMEM((D, tt), F32), pltpu.VMEM((et, tt), BF16), pltpu.VMEM((et, tt), BF16),
                        pltpu.VMEM((et, tt), BF16), pltpu.VMEM((et, tt), BF16)],
        compiler_params=_params("parallel", "arbitrary"),
    )(xb, u, vt, rk, w2, j1, g1)


def _ffn_ln_ple_kernel(x_ref, f_ref, p_ref, g_ref, b_ref, wg_ref, wp_ref, o_ref, ob_ref):
    x2 = _layer_norm(DN_ALPHA * x_ref[...] + f_ref[...], g_ref[...], b_ref[...])
    gate = jax.nn.sigmoid(_dot(x2.astype(BF16), wg_ref[...]))
    pe = _dot(p_ref[...].astype(BF16), wp_ref[...])
    out = x2 + gate * pe
    o_ref[...] = out
    ob_ref[...] = out.astype(BF16)


def _ffn_ln_ple(x, f, p, g, b, wg, wp):
    T, D = x.shape
    tm = ROW_TILE
    return pl.pallas_call(
        _ffn_ln_ple_kernel,
        grid=(T // tm,),
        in_specs=[pl.BlockSpec((tm, D), lambda i: (i, 0)), pl.BlockSpec((tm, D), lambda i: (i, 0)),
                  pl.BlockSpec((tm, p.shape[1]), lambda i: (i, 0)),
                  _full(g.shape), _full(b.shape), _full(wg.shape), _full(wp.shape)],
        out_specs=[pl.BlockSpec((tm, D), lambda i: (i, 0))] * 2,
        out_shape=[jax.ShapeDtypeStruct((T, D), F32), jax.ShapeDtypeStruct((T, D), BF16)],
        compiler_params=_params("parallel"),
    )(x, f, p, g, b, wg, wp)


def _mla_kv_kernel(x_ref, wd_ref, g_ref, wkn_ref, wvt_ref, cos_ref, sin_ref, k_ref, vt_ref):
    ckr = _dot(x_ref[...], wd_ref[...])
    ckv = _rms_norm(ckr[:, :MLA_KV_RANK], g_ref[...]).astype(BF16)
    kn = _dot(ckv, wkn_ref[...])
    vt = _dot_nt(wvt_ref[...], ckv).astype(BF16)
    pad_row = lax.broadcasted_iota(jnp.int32, (MLA_V_EXT - MLA_V, vt.shape[1]), 0)
    ones_rows = jnp.where(pad_row == 0, 1.0, 0.0).astype(BF16)
    for h in range(MLA_HEADS):
        vt_ref[h * MLA_V_EXT:h * MLA_V_EXT + MLA_V, :] = vt[h * MLA_V:(h + 1) * MLA_V, :]
        vt_ref[h * MLA_V_EXT + MLA_V:(h + 1) * MLA_V_EXT, :] = ones_rows
    kr = (ckr[:, MLA_KV_RANK:MLA_KV_RANK + LANES] * cos_ref[...]
          + ckr[:, MLA_KV_RANK + LANES:MLA_KV_RANK + 2 * LANES] * sin_ref[...]).astype(BF16)
    for h in range(MLA_HEADS):
        k_ref[:, h * MLA_QK_PAD:h * MLA_QK_PAD + MLA_NOPE] = kn[:, h * MLA_NOPE:(h + 1) * MLA_NOPE].astype(BF16)
        k_ref[:, h * MLA_QK_PAD + MLA_NOPE:(h + 1) * MLA_QK_PAD] = kr


def _mla_kv(xb, wd, g, wkn, wvt, cos, sin):
    T, D = xb.shape
    tm = ROW_TILE
    return pl.pallas_call(
        _mla_kv_kernel,
        grid=(T // tm,),
        in_specs=[pl.BlockSpec((tm, D), lambda i: (i, 0)), _full(wd.shape), _full(g.shape),
                  _full(wkn.shape), _full(wvt.shape),
                  pl.BlockSpec((tm, LANES), lambda i: (i, 0)), pl.BlockSpec((tm, LANES), lambda i: (i, 0))],
        out_specs=[pl.BlockSpec((tm, MLA_HEADS * MLA_QK_PAD), lambda i: (i, 0)),
                   pl.BlockSpec((MLA_HEADS * MLA_V_EXT, tm), lambda i: (0, i))],
        out_shape=[jax.ShapeDtypeStruct((T, MLA_HEADS * MLA_QK_PAD), BF16),
                   jax.ShapeDtypeStruct((MLA_HEADS * MLA_V_EXT, T), BF16)],
        compiler_params=_params("parallel"),
    )(xb, wd, g, wkn, wvt, cos, sin)


def _mla_q_kernel(x_ref, wdq_ref, g_ref, wuqt_ref, cost_ref, sint_ref, qt_ref):
    scale = (MLA_NOPE + MLA_ROPE) ** -0.5 * math.log2(math.e)
    cq = _rms_norm(_dot(x_ref[...], wdq_ref[...]), g_ref[...]).astype(BF16)
    qa = _dot_nt(wuqt_ref[...], cq)
    rot0 = MLA_HEADS * MLA_QK_PAD
    for h in range(MLA_HEADS):
        base = h * MLA_QK_PAD
        qt_ref[base:base + MLA_NOPE, :] = (qa[base:base + MLA_NOPE, :] * scale).astype(BF16)
        rope = (qa[base + MLA_NOPE:base + MLA_QK_PAD, :] * cost_ref[...]
                + qa[rot0 + h * LANES:rot0 + (h + 1) * LANES, :] * sint_ref[...])
        qt_ref[base + MLA_NOPE:base + MLA_QK_PAD, :] = (rope * scale).astype(BF16)


def _mla_q(xb, wdq, g, wuqt, cost, sint):
    T, D = xb.shape
    tm = ROW_TILE
    return pl.pallas_call(
        _mla_q_kernel,
        grid=(T // tm,),
        in_specs=[pl.BlockSpec((tm, D), lambda i: (i, 0)), _full(wdq.shape), _full(g.shape), _full(wuqt.shape),
                  pl.BlockSpec((LANES, tm), lambda i: (0, i)), pl.BlockSpec((LANES, tm), lambda i: (0, i))],
        out_specs=pl.BlockSpec((MLA_HEADS * MLA_QK_PAD, tm), lambda i: (0, i)),
        out_shape=jax.ShapeDtypeStruct((MLA_HEADS * MLA_QK_PAD, T), BF16),
        compiler_params=_params("parallel"),
    )(xb, wdq, g, wuqt, cost, sint)


def _attn_kernel(qt_ref, k_ref, vt_ref, o_ref, m_ref, acc_ref, s_a, s_b):
    i = pl.program_id(1)
    tq = qt_ref.shape[1]
    m_ref[...] = jnp.full(m_ref.shape, NEG_INF, F32)
    acc_ref[...] = jnp.zeros_like(acc_ref)

    def scores(j, s_ref):
        k = k_ref[pl.ds(pl.multiple_of(j * tq, tq), tq), :]
        s_ref[...] = _dot(k, qt_ref[...])

    def absorb(j, s_ref, masked):
        vt = vt_ref[:, pl.ds(pl.multiple_of(j * tq, tq), tq)]
        s = s_ref[...]
        if masked:
            key = lax.broadcasted_iota(jnp.int32, s.shape, 0)
            qry = lax.broadcasted_iota(jnp.int32, s.shape, 1)
            s = jnp.where(key <= qry, s, NEG_INF)
        m_prev = m_ref[...]
        m_new = jnp.maximum(m_prev, jnp.max(s, axis=0, keepdims=True))
        p = jnp.exp2(s - m_new)
        alpha = jnp.exp2(m_prev - m_new)
        acc_ref[...] = alpha * acc_ref[...] + _dot(vt, p.astype(BF16))
        m_ref[...] = m_new

    scores(0, s_a)

    def pair(jj, carry):
        scores(2 * jj + 1, s_b)
        absorb(2 * jj, s_a, False)
        scores(2 * jj + 2, s_a)
        absorb(2 * jj + 1, s_b, False)
        return carry

    lax.fori_loop(0, i // 2, pair, 0)

    @pl.when(i % 2 == 0)
    def _():
        absorb(i, s_a, True)

    @pl.when(i % 2 == 1)
    def _():
        scores(i, s_b)
        absorb(i - 1, s_a, False)
        absorb(i, s_b, True)

    o_ref[...] = (acc_ref[:MLA_V, :] / acc_ref[MLA_V:MLA_V + 1, :]).astype(o_ref.dtype)


def _attention(qt, k, vt):
    T = k.shape[0]
    tq = ATTN_BLOCK
    return pl.pallas_call(
        _attn_kernel,
        grid=(MLA_HEADS, T // tq),
        in_specs=[pl.BlockSpec((MLA_QK_PAD, tq), lambda h, i: (h, i)),
                  pl.BlockSpec((T, MLA_QK_PAD), lambda h, i: (0, h)),
                  pl.BlockSpec((MLA_V_EXT, T), lambda h, i: (h, 0))],
        out_specs=pl.BlockSpec((MLA_V, tq), lambda h, i: (h, i)),
        out_shape=jax.ShapeDtypeStruct((MLA_HEADS * MLA_V, T), BF16),
        scratch_shapes=[pltpu.VMEM((1, tq), F32), pltpu.VMEM((MLA_V_EXT, tq), F32),
                        pltpu.VMEM((tq, tq), F32), pltpu.VMEM((tq, tq), F32)],
        compiler_params=_params("parallel", "arbitrary"),
    )(qt, k, vt)


def _pad_last(w, width):
    return jnp.pad(w, [(0, 0)] * (w.ndim - 1) + [(0, width - w.shape[-1])])


def _rotate_half_cols(w):
    half = w.shape[-1] // 2
    return jnp.concatenate([-w[..., half:], w[..., :half]], axis=-1)


def _mlstm_weights(w_in, b_if):
    D = D_MODEL
    H = ML_HEADS
    q_end = H * ML_QK
    k_end = 2 * q_end
    v_end = k_end + H * ML_V
    i_end = v_end + H
    f_end = i_end + H
    wq = _pad_last(w_in[:, :q_end].reshape(D, H, ML_QK), LANES).reshape(D, H * LANES)
    wk = _pad_last((w_in[:, q_end:k_end] * (ML_QK ** -0.5)).reshape(D, H, ML_QK), LANES).reshape(D, H * LANES)
    wqv = jnp.concatenate([wq, w_in[:, k_end:v_end]], axis=1).astype(BF16)
    wog = w_in[:, f_end:].astype(BF16)
    wgate = w_in[:, v_end:f_end]
    wg = _pad_last(wgate, LANES).astype(BF16)
    wkt = wk.T.astype(BF16)
    wgt = wgate.T.astype(BF16)
    bias = jnp.concatenate([b_if[0], b_if[1]]).astype(F32)
    bias_cols = _pad_last(bias[None, :], LANES)
    bias_rows = jnp.broadcast_to(bias[:, None], (2 * H, ML_CHUNK))
    return wqv, wog, wg, wkt, wgt, bias_cols, bias_rows


def _mla_weights(kv_w_down, kv_w_up, w_uq):
    H = MLA_HEADS
    wr = kv_w_down[:, MLA_KV_RANK:]
    wd = jnp.concatenate([kv_w_down[:, :MLA_KV_RANK], _pad_last(wr, LANES),
                          _pad_last(_rotate_half_cols(wr), LANES)], axis=1).astype(BF16)
    up = kv_w_up.reshape(MLA_KV_RANK, H, MLA_NOPE + MLA_V)
    wkn = up[:, :, :MLA_NOPE].reshape(MLA_KV_RANK, H * MLA_NOPE).astype(BF16)
    wvt = up[:, :, MLA_NOPE:].reshape(MLA_KV_RANK, H * MLA_V).T.astype(BF16)
    uq = w_uq.reshape(MLA_Q_RANK, H, MLA_NOPE + MLA_ROPE)
    w_a = _pad_last(uq, MLA_QK_PAD).reshape(MLA_Q_RANK, H * MLA_QK_PAD)
    w_r = _pad_last(_rotate_half_cols(uq[:, :, MLA_NOPE:]), LANES).reshape(MLA_Q_RANK, H * LANES)
    wuqt = jnp.concatenate([w_a, w_r], axis=1).T.astype(BF16)
    return wd, wkn, wvt, wuqt


def _rope_tables(positions):
    inv_freq = ROPE_THETA ** (-jnp.arange(0, MLA_ROPE, 2, dtype=F32) / MLA_ROPE)
    ang = positions.astype(F32)[:, None] * inv_freq
    cos = jnp.cos(ang)
    sin = jnp.sin(ang)
    cos = _pad_last(jnp.concatenate([cos, cos], axis=-1), LANES)
    sin = _pad_last(jnp.concatenate([sin, sin], axis=-1), LANES)
    return cos, sin, cos.T, sin.T


def _peer_block(x, xb, p, w_q, sub_keys, u, v, ln_g, ln_b, w_gate, w_proj):
    wq = w_q.astype(BF16)
    keys = sub_keys.reshape(2 * PEER_HEADS, PEER_KEYS, PEER_HALF).astype(BF16)
    rk, w2, j1, g1 = _peer_route(x, wq, keys)
    vt = jnp.swapaxes(v.astype(BF16).reshape(-1, EXPERT_TILE, v.shape[1]), 1, 2)
    ffn = _peer_expert(xb, u.astype(BF16), vt, rk, w2, j1, g1)
    return _ffn_ln_ple(x, ffn, p, ln_g[None, :], ln_b[None, :], w_gate.astype(BF16), w_proj.astype(BF16))


def kernel(x, p, positions, ln_g, ln_b, a_w_in, a_b_if, a_hn_g, a_w_out, kv_w_down, kv_norm_g, kv_w_up,
           b_w_dq, b_q_norm_g, b_w_uq, b_w_out, peer_w_q, peer_sub_keys, peer_u, peer_v, ple_w_proj,
           ple_w_gate):
    B, S, D = x.shape
    xs = x.reshape(B * S, D)
    ps = p.reshape(DEPTH, B * S, PLE_DIM)
    cos, sin, cos_t, sin_t = _rope_tables(positions.reshape(B * S))

    wqv, wog, wg, wkt, wgt, bias_cols, bias_rows = _mlstm_weights(a_w_in[0], a_b_if[0])
    qv, og, g, kt, gt = _mlstm_proj(xs, wqv, wog, wg, wkt, wgt)
    h = _mlstm_scan(qv, kt, g, gt, og, bias_cols, bias_rows, a_hn_g[0][None, :])
    xs, xb = _mix_ln(h, a_w_out[0].astype(BF16), xs, ln_g[0, 0][None, :], ln_b[0, 0][None, :])
    xs, xb = _peer_block(xs, xb, ps[0], peer_w_q[0], peer_sub_keys[0], peer_u[0], peer_v[0],
                         ln_g[0, 1], ln_b[0, 1], ple_w_gate[0], ple_w_proj[0])

    wd, wkn, wvt, wuqt = _mla_weights(kv_w_down, kv_w_up, b_w_uq[0])
    k_all, vt_all = _mla_kv(xb, wd, kv_norm_g[None, :], wkn, wvt, cos, sin)
    qt_all = _mla_q(xb, b_w_dq[0].astype(BF16), b_q_norm_g[0][None, :], wuqt, cos_t, sin_t)
    ot = _attention(qt_all, k_all, vt_all)
    xs, xb = _mix_ln(ot, b_w_out[0].astype(BF16), xs, ln_g[1, 0][None, :], ln_b[1, 0][None, :],
                     a_feature_major=True)
    xs, _ = _peer_block(xs, xb, ps[1], peer_w_q[1], peer_sub_keys[1], peer_u[1], peer_v[1],
                        ln_g[1, 1], ln_b[1, 1], ple_w_gate[1], ple_w_proj[1])
    return xs.reshape(B, S, D)
```

```python
import functools
import math

import jax
import jax.numpy as jnp
from jax import lax
from jax.experimental import pallas as pl
from jax.experimental.pallas import tpu as pltpu

F32 = jnp.float32
BF16 = jnp.bfloat16

LANES = 128
D_MODEL = 1024
DEPTH = 2
DN_ALPHA = (2.0 * DEPTH) ** 0.25
NORM_EPS = 1e-5

ML_HEADS = 8
ML_QK = 64
ML_V = 128
ML_CHUNK = 256

MLA_HEADS = 8
MLA_NOPE = 128
MLA_ROPE = 64
MLA_V = 128
MLA_V_EXT = 144
MLA_KV_RANK = 256
MLA_Q_RANK = 384
MLA_QK_PAD = 256
ROPE_THETA = 10000.0
ATTN_BLOCK = 1024

PEER_HEADS = 8
PEER_KEYS = 128
PEER_HALF = 128
PEER_TOPK = 16
PLE_DIM = 256

ROW_TILE = 512
ROUTE_TILE = 512
ROUTE_GROUP = 256
EXPERT_TOK_TILE = 1024
EXPERT_TILE = 1024
VMEM_LIMIT = 56 * 1024 * 1024

NEG_INF = float("-inf")


def _params(*sem):
    return pltpu.CompilerParams(dimension_semantics=sem, vmem_limit_bytes=VMEM_LIMIT)


def _dot(a, b):
    return jnp.dot(a, b, preferred_element_type=F32)


def _dot_nt(a, b):
    return lax.dot_general(a, b, (((1,), (1,)), ((), ())), preferred_element_type=F32)


def _layer_norm(y, g, b):
    mu = jnp.mean(y, axis=-1, keepdims=True)
    yc = y - mu
    var = jnp.mean(yc * yc, axis=-1, keepdims=True)
    return yc * lax.rsqrt(var + NORM_EPS) * g + b


def _rms_norm(y, g):
    return y * lax.rsqrt(jnp.mean(y * y, axis=-1, keepdims=True) + NORM_EPS) * g


def _full(shape):
    nd = len(shape)
    return pl.BlockSpec(shape, lambda *_: (0,) * nd)


def _mlstm_proj_kernel(x_ref, wqv_ref, wog_ref, wg_ref, wkt_ref, wgt_ref,
                       qv_ref, og_ref, g_ref, kt_ref, gt_ref):
    xb = x_ref[...].astype(BF16)
    qv_ref[...] = _dot(xb, wqv_ref[...]).astype(BF16)
    og_ref[...] = _dot(xb, wog_ref[...])
    g_ref[...] = _dot(xb, wg_ref[...])
    kt_ref[...] = _dot_nt(wkt_ref[...], xb).astype(BF16)
    gt_ref[...] = _dot_nt(wgt_ref[...], xb)


def _mlstm_proj(x, wqv, wog, wg, wkt, wgt):
    T, D = x.shape
    tm = ROW_TILE
    nqv, nk, ng = wqv.shape[1], wkt.shape[0], wgt.shape[0]
    return pl.pallas_call(
        _mlstm_proj_kernel,
        grid=(T // tm,),
        in_specs=[pl.BlockSpec((tm, D), lambda i: (i, 0)),
                  _full(wqv.shape), _full(wog.shape), _full(wg.shape), _full(wkt.shape), _full(wgt.shape)],
        out_specs=[pl.BlockSpec((tm, nqv), lambda i: (i, 0)),
                   pl.BlockSpec((tm, D), lambda i: (i, 0)),
                   pl.BlockSpec((tm, LANES), lambda i: (i, 0)),
                   pl.BlockSpec((nk, tm), lambda i: (0, i)),
                   pl.BlockSpec((ng, tm), lambda i: (0, i))],
        out_shape=[jax.ShapeDtypeStruct((T, nqv), BF16),
                   jax.ShapeDtypeStruct((T, D), F32),
                   jax.ShapeDtypeStruct((T, LANES), F32),
                   jax.ShapeDtypeStruct((nk, T), BF16),
                   jax.ShapeDtypeStruct((ng, T), F32)],
        compiler_params=_params("parallel"),
    )(x, wqv, wog, wg, wkt, wgt)


def _split3(x):
    hi = x.astype(BF16)
    r1 = x - hi.astype(F32)
    mid = r1.astype(BF16)
    lo = (r1 - mid.astype(F32)).astype(BF16)
    return hi, mid, lo


def _log_sigmoid(x):
    return jnp.minimum(x, 0.0) - jnp.log1p(jnp.exp(-jnp.abs(x)))


def _mlstm_scan_kernel(q_ref, v_ref, kt_ref, g_ref, gt_ref, og_ref, bc_ref, br_ref, hng_ref,
                       o_ref, c_ref, m_ref):
    H = ML_HEADS
    L = q_ref.shape[0]

    @pl.when(pl.program_id(0) == 0)
    def _():
        c_ref[...] = jnp.zeros_like(c_ref)
        m_ref[...] = jnp.zeros_like(m_ref)

    row = lax.broadcasted_iota(jnp.int32, (L, L), 0)
    col = lax.broadcasted_iota(jnp.int32, (L, L), 1)
    causal = col <= row
    tril = jnp.where(causal, 1.0, 0.0).astype(BF16)
    triu = jnp.where(row <= col, 1.0, 0.0).astype(BF16)

    g = g_ref[...] + bc_ref[...]
    gt = gt_ref[...] + br_ref[...]
    b_cols = sum(_dot(tril, piece) for piece in _split3(_log_sigmoid(g)))
    b_rows = sum(_dot(piece, triu) for piece in _split3(_log_sigmoid(gt)))

    lane = lax.broadcasted_iota(jnp.int32, (L, LANES), 1)
    ones_col = jnp.where(lane == 0, 1.0, 0.0).astype(BF16)

    for h in range(H):
        b_c = b_cols[:, H + h:H + h + 1]
        b_r = b_rows[H + h:H + h + 1, :]
        ig_r = gt[h:h + 1, :]
        m_prev = m_ref[h:h + 1, 0:1]
        dmat = jnp.where(causal, b_c - b_r + ig_r, NEG_INF)
        log_inter = b_c + m_prev
        m_t = jnp.maximum(log_inter, jnp.max(dmat, axis=1, keepdims=True))
        w = jnp.exp(dmat - m_t)
        s_inter = jnp.exp(log_inter - m_t)

        q = q_ref[:, h * LANES:(h + 1) * LANES]
        kt = kt_ref[h * LANES:(h + 1) * LANES, :]
        v_ext = jnp.concatenate([v_ref[:, h * ML_V:(h + 1) * ML_V], ones_col], axis=1)
        s = (_dot(q, kt) * w).astype(BF16)
        c_prev = c_ref[h]
        nd = _dot(s, v_ext) + s_inter * _dot(q, c_prev.astype(BF16))
        num = nd[:, :ML_V]
        den = nd[:, ML_V:ML_V + 1]
        hh = num / jnp.maximum(jnp.abs(den), jnp.exp(-m_t))

        b_last = b_r[:, L - 1:L]
        log_src = b_last - b_r + ig_r
        m_new = jnp.maximum(b_last + m_prev, jnp.max(log_src, axis=1, keepdims=True))
        w_src = jnp.exp(log_src - m_new)
        s_old = jnp.exp(b_last + m_prev - m_new)
        c_ref[h] = s_old * c_prev + _dot((kt.astype(F32) * w_src).astype(BF16), v_ext)
        m_ref[h:h + 1, :] = jnp.broadcast_to(m_new, (1, LANES))

        mu = jnp.mean(hh, axis=-1, keepdims=True)
        hc = hh - mu
        var = jnp.mean(hc * hc, axis=-1, keepdims=True)
        hn = hc * lax.rsqrt(var + NORM_EPS) * hng_ref[:, h * ML_V:(h + 1) * ML_V]
        og = og_ref[:, h * ML_V:(h + 1) * ML_V]
        o_ref[:, h * ML_V:(h + 1) * ML_V] = (jax.nn.sigmoid(og) * hn).astype(o_ref.dtype)


def _mlstm_scan(qv, kt, g, gt, og, bias_cols, bias_rows, hn_g):
    T = qv.shape[0]
    L = ML_CHUNK
    D = D_MODEL
    return pl.pallas_call(
        _mlstm_scan_kernel,
        grid=(T // L,),
        in_specs=[pl.BlockSpec((L, D), lambda c: (c, 0)),
                  pl.BlockSpec((L, D), lambda c: (c, 1)),
                  pl.BlockSpec((D, L), lambda c: (0, c)),
                  pl.BlockSpec((L, LANES), lambda c: (c, 0)),
                  pl.BlockSpec((2 * ML_HEADS, L), lambda c: (0, c)),
                  pl.BlockSpec((L, D), lambda c: (c, 0)),
                  _full(bias_cols.shape), _full(bias_rows.shape), _full(hn_g.shape)],
        out_specs=pl.BlockSpec((L, D), lambda c: (c, 0)),
        out_shape=jax.ShapeDtypeStruct((T, D), BF16),
        scratch_shapes=[pltpu.VMEM((ML_HEADS, LANES, 2 * LANES), F32),
                        pltpu.VMEM((ML_HEADS, LANES), F32)],
        compiler_params=_params("arbitrary"),
    )(qv, qv, kt, g, gt, og, bias_cols, bias_rows, hn_g)


def _mix_ln_kernel(a_ref, w_ref, x_ref, g_ref, b_ref, o_ref, ob_ref, *, a_feature_major):
    if a_feature_major:
        mix = lax.dot_general(a_ref[...], w_ref[...], (((0,), (0,)), ((), ())), preferred_element_type=F32)
    else:
        mix = _dot(a_ref[...], w_ref[...])
    out = _layer_norm(mix + DN_ALPHA * x_ref[...], g_ref[...], b_ref[...])
    o_ref[...] = out
    ob_ref[...] = out.astype(BF16)


def _mix_ln(a, w, x, g, b, *, a_feature_major=False):
    T, D = x.shape
    tm = ROW_TILE
    if a_feature_major:
        a_spec = pl.BlockSpec((a.shape[0], tm), lambda i: (0, i))
    else:
        a_spec = pl.BlockSpec((tm, a.shape[1]), lambda i: (i, 0))
    return pl.pallas_call(
        functools.partial(_mix_ln_kernel, a_feature_major=a_feature_major),
        grid=(T // tm,),
        in_specs=[a_spec, _full(w.shape),
                  pl.BlockSpec((tm, D), lambda i: (i, 0)), _full(g.shape), _full(b.shape)],
        out_specs=[pl.BlockSpec((tm, D), lambda i: (i, 0))] * 2,
        out_shape=[jax.ShapeDtypeStruct((T, D), F32), jax.ShapeDtypeStruct((T, D), BF16)],
        compiler_params=_params("parallel"),
    )(a, w, x, g, b)


PEER_NTOP = PEER_TOPK + 1
PEER_TOP_ROWS = 24


def _peer_candidate_blocks(a_ref, b_ref):
    blocks = [a_ref[...] + b_ref[0:1, :], b_ref[...] + a_ref[0:1, :]]
    for j in range(1, 8):
        blocks.append(a_ref[0:8, :] + b_ref[j:j + 1, :])
    return jnp.concatenate(blocks, axis=0)


def _bf16_pair_words(x):
    bits = lax.bitcast_convert_type(x, jnp.uint32)
    r = (bits + jnp.uint32(0x7FFF) + ((bits >> 16) & jnp.uint32(1))) >> 16
    return r | (r << 16)


def _peer_route_kernel(x_ref, wq_ref, keys_ref, rk_ref, w2_ref, j1_ref, g1_ref, s_ref, top_ref):
    q = _dot(x_ref[...].astype(BF16), wq_ref[...]).astype(BF16)
    for hc in range(2 * PEER_HEADS):
        s_ref[hc] = _dot_nt(keys_ref[hc], q[:, hc * PEER_HALF:(hc + 1) * PEER_HALF])
    top_ref[...] = jnp.full(top_ref.shape, NEG_INF, F32)
    n_groups = s_ref.shape[2] // ROUTE_GROUP

    def head_group(idx, carry):
        h = idx // n_groups
        ls = pl.ds(pl.multiple_of((idx % n_groups) * ROUTE_GROUP, ROUTE_GROUP), ROUTE_GROUP)
        halves = []
        for c in range(2):
            s = s_ref[2 * h + c, :, ls]
            cur = s
            rank = jnp.full(s.shape, float(PEER_KEYS - 1), F32)
            for r in range(PEER_NTOP):
                mx = jnp.max(cur, axis=0, keepdims=True)
                top_ref[c, r:r + 1, :] = mx
                hit = cur == mx
                if c == 1:
                    rank = jnp.where(hit, float(r), rank)
                cur = jnp.where(hit, NEG_INF, cur)
            halves.append((s, rank))
        (s1, _), (s2, rank2) = halves
        cand = _peer_candidate_blocks(top_ref.at[0], top_ref.at[1])
        a0 = top_ref[0, 0:1, :]
        b0 = top_ref[1, 0:1, :]
        best = a0 + b0
        z = jnp.zeros_like(best)
        t_prev = best
        t_cur = best
        for r in range(PEER_NTOP):
            t_prev = t_cur
            t_cur = jnp.max(cand, axis=0, keepdims=True)
            if r < PEER_TOPK:
                z = z + jnp.exp(t_cur - best)
            cand = jnp.where(cand == t_cur, NEG_INF, cand)
        tau = 0.5 * (t_prev + t_cur)
        thr = tau - s1
        count = jnp.zeros_like(s1)
        for r in range(PEER_NTOP):
            count = jnp.where(top_ref[1, r:r + 1, :] >= thr, float(r + 1), count)
        rk_ref[h, :, ls] = pltpu.bitcast(rank2.astype(BF16), jnp.uint32)
        w2_ref[h, :, ls] = pltpu.bitcast(jnp.exp(s2 - b0).astype(BF16), jnp.uint32)
        j1_ref[h, :, ls] = _bf16_pair_words(count)
        g1_ref[h, :, ls] = _bf16_pair_words(jnp.exp(s1 - a0) / z)
        return carry

    lax.fori_loop(0, PEER_HEADS * n_groups, head_group, 0)


def _peer_route(x, wq, keys):
    T, D = x.shape
    tt = ROUTE_TILE
    ospec = pl.BlockSpec((PEER_HEADS, PEER_KEYS, tt), lambda i: (0, 0, i))
    return pl.pallas_call(
        _peer_route_kernel,
        grid=(T // tt,),
        in_specs=[pl.BlockSpec((tt, D), lambda i: (i, 0)), _full(wq.shape), _full(keys.shape)],
        out_specs=[pl.BlockSpec((PEER_HEADS, PEER_KEYS // 2, tt), lambda i: (0, 0, i))] * 2 + [ospec] * 2,
        out_shape=[jax.ShapeDtypeStruct((PEER_HEADS, PEER_KEYS // 2, T), jnp.uint32)] * 2
        + [jax.ShapeDtypeStruct((PEER_HEADS, PEER_KEYS, T), jnp.uint32)] * 2,
        scratch_shapes=[pltpu.VMEM((2 * PEER_HEADS, PEER_KEYS, tt), F32), pltpu.VMEM((2, PEER_TOP_ROWS, ROUTE_GROUP), F32)],
        compiler_params=_params("parallel"),
    )(x, wq, keys)


def _gelu(x):
    return 0.5 * x * (1.0 + lax.erf(x * (1.0 / math.sqrt(2.0))))


GATE_ROWS = 64
GATE_KEYS = 8


def _row_bf16(words, rows):
    return pltpu.bitcast(jnp.broadcast_to(words, (rows // 2, words.shape[1])), BF16)


def _peer_expert_kernel(x_ref, u_ref, vt_ref, rk_ref, w2_ref, j1_ref, g1_ref, o_ref,
                        acc_ref, act0, act1, p0, p1, *, n_tiles):
    s = pl.program_id(1)
    last = pl.num_programs(1) - 1
    tt = x_ref.shape[0]
    n_sub = u_ref.shape[0] // PEER_KEYS
    n_rb = PEER_KEYS // GATE_ROWS

    @pl.when(s == 0)
    def _():
        acc_ref[...] = jnp.zeros_like(acc_ref)
        act0[...] = _dot_nt(u_ref[...], x_ref[...])

    def gate_block(act_ref, p_ref, lc, rb, kb):
        ls = slice(lc * LANES, (lc + 1) * LANES)
        ws = slice(rb * (GATE_ROWS // 2), (rb + 1) * (GATE_ROWS // 2))
        keys = range(kb * GATE_KEYS, (kb + 1) * GATE_KEYS)
        gates = {}
        for h in range(PEER_HEADS):
            rk = pltpu.bitcast(rk_ref[h, ws, ls], BF16)
            w = pltpu.bitcast(w2_ref[h, ws, ls], BF16)
            for a in keys:
                count = _row_bf16(j1_ref[h, a:a + 1, ls], GATE_ROWS)
                g = _row_bf16(g1_ref[h, a:a + 1, ls], GATE_ROWS)
                term = jnp.where(rk < count, w, jnp.zeros_like(w)) * g
                gates[a] = term if h == 0 else gates[a] + term
        for a in keys:
            es = slice(a * PEER_KEYS + rb * GATE_ROWS, a * PEER_KEYS + (rb + 1) * GATE_ROWS)
            p_ref[es, ls] = gates[a] * _gelu(act_ref[es, ls].astype(BF16))

    def stages(parity, first=True, second=True):
        act_w, act_r, p_w, p_r = (act0, act1, p1, p0) if parity == 0 else (act1, act0, p0, p1)
        if second:
            acc_ref[...] += _dot(vt_ref[...], p_r[...])
        if first:
            act_w[...] = _dot_nt(u_ref[...], x_ref[...])
        for lc in range(tt // LANES):
            for rb in range(n_rb):
                for kb in range(n_sub // GATE_KEYS):
                    gate_block(act_r, p_w, lc, rb, kb)

    assert n_tiles % 2 == 0 and n_tiles >= 4
    steady = jnp.logical_and(s > 1, s < n_tiles)

    @pl.when(s == 1)
    def _():
        stages(1, second=False)

    @pl.when(jnp.logical_and(steady, s % 2 == 0))
    def _():
        stages(0)

    @pl.when(jnp.logical_and(steady, s % 2 == 1))
    def _():
        stages(1)

    @pl.when(s == n_tiles)
    def _():
        stages(0, first=False)

    @pl.when(s == last)
    def _():
        p_last = p1 if n_tiles % 2 == 0 else p0
        o_ref[...] = (acc_ref[...] + _dot(vt_ref[...], p_last[...])).T


def _peer_expert(xb, u, vt, rk, w2, j1, g1):
    T, D = xb.shape
    E = u.shape[0]
    tt, et = EXPERT_TOK_TILE, EXPERT_TILE
    n_e = E // et
    rspec = pl.BlockSpec((PEER_HEADS, PEER_KEYS // 2, tt), lambda i, s: (0, 0, i))
    sspec = pl.BlockSpec((PEER_HEADS, et // PEER_KEYS, tt),
                         lambda i, s: (0, jnp.clip(s - 1, 0, n_e - 1), i))
    return pl.pallas_call(
        functools.partial(_peer_expert_kernel, n_tiles=n_e),
        grid=(T // tt, n_e + 2),
        in_specs=[pl.BlockSpec((tt, D), lambda i, s: (i, 0)),
                  pl.BlockSpec((et, D), lambda i, s: (jnp.minimum(s, n_e - 1), 0)),
                  pl.BlockSpec((None, D, et), lambda i, s: (jnp.clip(s - 2, 0, n_e - 1), 0, 0)),
                  rspec, rspec, sspec, sspec],
        out_specs=pl.BlockSpec((tt, D), lambda i, s: (i, 0)),
        out_shape=jax.ShapeDtypeStruct((T, D), F32),
        scratch_shapes=[pltpu.VMEM((D, tt), F32), pltpu.VMEM((et, tt), F32), pltpu.VMEM((et, tt), F32),
                        pltpu.VMEM((et, tt), BF16), pltpu.VMEM((et, tt), BF16)],
        compiler_params=_params("parallel", "arbitrary"),
    )(xb, u, vt, rk, w2, j1, g1)


def _ffn_ln_ple_kernel(x_ref, f_ref, p_ref, g_ref, b_ref, wg_ref, wp_ref, o_ref, ob_ref):
    x2 = _layer_norm(DN_ALPHA * x_ref[...] + f_ref[...], g_ref[...], b_ref[...])
    gate = jax.nn.sigmoid(_dot(x2.astype(BF16), wg_ref[...]))
    pe = _dot(p_ref[...].astype(BF16), wp_ref[...])
    out = x2 + gate * pe
    o_ref[...] = out
    ob_ref[...] = out.astype(BF16)


def _ffn_ln_ple(x, f, p, g, b, wg, wp):
    T, D = x.shape
    tm = ROW_TILE
    return pl.pallas_call(
        _ffn_ln_ple_kernel,
        grid=(T // tm,),
        in_specs=[pl.BlockSpec((tm, D), lambda i: (i, 0)), pl.BlockSpec((tm, D), lambda i: (i, 0)),
                  pl.BlockSpec((tm, p.shape[1]), lambda i: (i, 0)),
                  _full(g.shape), _full(b.shape), _full(wg.shape), _full(wp.shape)],
        out_specs=[pl.BlockSpec((tm, D), lambda i: (i, 0))] * 2,
        out_shape=[jax.ShapeDtypeStruct((T, D), F32), jax.ShapeDtypeStruct((T, D), BF16)],
        compiler_params=_params("parallel"),
    )(x, f, p, g, b, wg, wp)


def _mla_kv_kernel(x_ref, wd_ref, g_ref, wkn_ref, wvt_ref, cos_ref, sin_ref, k_ref, vt_ref):
    ckr = _dot(x_ref[...], wd_ref[...])
    ckv = _rms_norm(ckr[:, :MLA_KV_RANK], g_ref[...]).astype(BF16)
    kn = _dot(ckv, wkn_ref[...])
    vt = _dot_nt(wvt_ref[...], ckv).astype(BF16)
    pad_row = lax.broadcasted_iota(jnp.int32, (MLA_V_EXT - MLA_V, vt.shape[1]), 0)
    ones_rows = jnp.where(pad_row == 0, 1.0, 0.0).astype(BF16)
    for h in range(MLA_HEADS):
        vt_ref[h * MLA_V_EXT:h * MLA_V_EXT + MLA_V, :] = vt[h * MLA_V:(h + 1) * MLA_V, :]
        vt_ref[h * MLA_V_EXT + MLA_V:(h + 1) * MLA_V_EXT, :] = ones_rows
    kr = (ckr[:, MLA_KV_RANK:MLA_KV_RANK + LANES] * cos_ref[...]
          + ckr[:, MLA_KV_RANK + LANES:MLA_KV_RANK + 2 * LANES] * sin_ref[...]).astype(BF16)
    for h in range(MLA_HEADS):
        k_ref[:, h * MLA_QK_PAD:h * MLA_QK_PAD + MLA_NOPE] = kn[:, h * MLA_NOPE:(h + 1) * MLA_NOPE].astype(BF16)
        k_ref[:, h * MLA_QK_PAD + MLA_NOPE:(h + 1) * MLA_QK_PAD] = kr


def _mla_kv(xb, wd, g, wkn, wvt, cos, sin):
    T, D = xb.shape
    tm = ROW_TILE
    return pl.pallas_call(
        _mla_kv_kernel,
        grid=(T // tm,),
        in_specs=[pl.BlockSpec((tm, D), lambda i: (i, 0)), _full(wd.shape), _full(g.shape),
                  _full(wkn.shape), _full(wvt.shape),
                  pl.BlockSpec((tm, LANES), lambda i: (i, 0)), pl.BlockSpec((tm, LANES), lambda i: (i, 0))],
        out_specs=[pl.BlockSpec((tm, MLA_HEADS * MLA_QK_PAD), lambda i: (i, 0)),
                   pl.BlockSpec((MLA_HEADS * MLA_V_EXT, tm), lambda i: (0, i))],
        out_shape=[jax.ShapeDtypeStruct((T, MLA_HEADS * MLA_QK_PAD), BF16),
                   jax.ShapeDtypeStruct((MLA_HEADS * MLA_V_EXT, T), BF16)],
        compiler_params=_params("parallel"),
    )(xb, wd, g, wkn, wvt, cos, sin)


def _mla_q_kernel(x_ref, wdq_ref, g_ref, wuqt_ref, cost_ref, sint_ref, qt_ref):
    scale = (MLA_NOPE + MLA_ROPE) ** -0.5 * math.log2(math.e)
    cq = _rms_norm(_dot(x_ref[...], wdq_ref[...]), g_ref[...]).astype(BF16)
    qa = _dot_nt(wuqt_ref[...], cq)
    rot0 = MLA_HEADS * MLA_QK_PAD
    for h in range(MLA_HEADS):
        base = h * MLA_QK_PAD
        qt_ref[base:base + MLA_NOPE, :] = (qa[base:base + MLA_NOPE, :] * scale).astype(BF16)
        rope = (qa[base + MLA_NOPE:base + MLA_QK_PAD, :] * cost_ref[...]
                + qa[rot0 + h * LANES:rot0 + (h + 1) * LANES, :] * sint_ref[...])
        qt_ref[base + MLA_NOPE:base + MLA_QK_PAD, :] = (rope * scale).astype(BF16)


def _mla_q(xb, wdq, g, wuqt, cost, sint):
    T, D = xb.shape
    tm = ROW_TILE
    return pl.pallas_call(
        _mla_q_kernel,
        grid=(T // tm,),
        in_specs=[pl.BlockSpec((tm, D), lambda i: (i, 0)), _full(wdq.shape), _full(g.shape), _full(wuqt.shape),
                  pl.BlockSpec((LANES, tm), lambda i: (0, i)), pl.BlockSpec((LANES, tm), lambda i: (0, i))],
        out_specs=pl.BlockSpec((MLA_HEADS * MLA_QK_PAD, tm), lambda i: (0, i)),
        out_shape=jax.ShapeDtypeStruct((MLA_HEADS * MLA_QK_PAD, T), BF16),
        compiler_params=_params("parallel"),
    )(xb, wdq, g, wuqt, cost, sint)


def _attn_kernel(qt_ref, k_ref, vt_ref, o_ref, m_ref, acc_ref, s_a, s_b):
    i = pl.program_id(1)
    tq = qt_ref.shape[1]
    m_ref[...] = jnp.full(m_ref.shape, NEG_INF, F32)
    acc_ref[...] = jnp.zeros_like(acc_ref)

    def scores(j, s_ref):
        k = k_ref[pl.ds(pl.multiple_of(j * tq, tq), tq), :]
        s_ref[...] = _dot(k, qt_ref[...])

    def absorb(j, s_ref, masked):
        vt = vt_ref[:, pl.ds(pl.multiple_of(j * tq, tq), tq)]
        s = s_ref[...]
        if masked:
            key = lax.broadcasted_iota(jnp.int32, s.shape, 0)
            qry = lax.broadcasted_iota(jnp.int32, s.shape, 1)
            s = jnp.where(key <= qry, s, NEG_INF)
        m_prev = m_ref[...]
        m_new = jnp.maximum(m_prev, jnp.max(s, axis=0, keepdims=True))
        p = jnp.exp2(s - m_new)
        alpha = jnp.exp2(m_prev - m_new)
        acc_ref[...] = alpha * acc_ref[...] + _dot(vt, p.astype(BF16))
        m_ref[...] = m_new

    scores(0, s_a)

    def pair(jj, carry):
        scores(2 * jj + 1, s_b)
        absorb(2 * jj, s_a, False)
        scores(2 * jj + 2, s_a)
        absorb(2 * jj + 1, s_b, False)
        return carry

    lax.fori_loop(0, i // 2, pair, 0)

    @pl.when(i % 2 == 0)
    def _():
        absorb(i, s_a, True)

    @pl.when(i % 2 == 1)
    def _():
        scores(i, s_b)
        absorb(i - 1, s_a, False)
        absorb(i, s_b, True)

    o_ref[...] = (acc_ref[:MLA_V, :] / acc_ref[MLA_V:MLA_V + 1, :]).astype(o_ref.dtype)


def _attention(qt, k, vt):
    T = k.shape[0]
    tq = ATTN_BLOCK
    return pl.pallas_call(
        _attn_kernel,
        grid=(MLA_HEADS, T // tq),
        in_specs=[pl.BlockSpec((MLA_QK_PAD, tq), lambda h, i: (h, i)),
                  pl.BlockSpec((T, MLA_QK_PAD), lambda h, i: (0, h)),
                  pl.BlockSpec((MLA_V_EXT, T), lambda h, i: (h, 0))],
        out_specs=pl.BlockSpec((MLA_V, tq), lambda h, i: (h, i)),
        out_shape=jax.ShapeDtypeStruct((MLA_HEADS * MLA_V, T), BF16),
        scratch_shapes=[pltpu.VMEM((1, tq), F32), pltpu.VMEM((MLA_V_EXT, tq), F32),
                        pltpu.VMEM((tq, tq), F32), pltpu.VMEM((tq, tq), F32)],
        compiler_params=_params("parallel", "arbitrary"),
    )(qt, k, vt)


def _pad_last(w, width):
    return jnp.pad(w, [(0, 0)] * (w.ndim - 1) + [(0, width - w.shape[-1])])


def _rotate_half_cols(w):
    half = w.shape[-1] // 2
    return jnp.concatenate([-w[..., half:], w[..., :half]], axis=-1)


def _mlstm_weights(w_in, b_if):
    D = D_MODEL
    H = ML_HEADS
    q_end = H * ML_QK
    k_end = 2 * q_end
    v_end = k_end + H * ML_V
    i_end = v_end + H
    f_end = i_end + H
    wq = _pad_last(w_in[:, :q_end].reshape(D, H, ML_QK), LANES).reshape(D, H * LANES)
    wk = _pad_last((w_in[:, q_end:k_end] * (ML_QK ** -0.5)).reshape(D, H, ML_QK), LANES).reshape(D, H * LANES)
    wqv = jnp.concatenate([wq, w_in[:, k_end:v_end]], axis=1).astype(BF16)
    wog = w_in[:, f_end:].astype(BF16)
    wgate = w_in[:, v_end:f_end]
    wg = _pad_last(wgate, LANES).astype(BF16)
    wkt = wk.T.astype(BF16)
    wgt = wgate.T.astype(BF16)
    bias = jnp.concatenate([b_if[0], b_if[1]]).astype(F32)
    bias_cols = _pad_last(bias[None, :], LANES)
    bias_rows = jnp.broadcast_to(bias[:, None], (2 * H, ML_CHUNK))
    return wqv, wog, wg, wkt, wgt, bias_cols, bias_rows


def _mla_weights(kv_w_down, kv_w_up, w_uq):
    H = MLA_HEADS
    wr = kv_w_down[:, MLA_KV_RANK:]
    wd = jnp.concatenate([kv_w_down[:, :MLA_KV_RANK], _pad_last(wr, LANES),
                          _pad_last(_rotate_half_cols(wr), LANES)], axis=1).astype(BF16)
    up = kv_w_up.reshape(MLA_KV_RANK, H, MLA_NOPE + MLA_V)
    wkn = up[:, :, :MLA_NOPE].reshape(MLA_KV_RANK, H * MLA_NOPE).astype(BF16)
    wvt = up[:, :, MLA_NOPE:].reshape(MLA_KV_RANK, H * MLA_V).T.astype(BF16)
    uq = w_uq.reshape(MLA_Q_RANK, H, MLA_NOPE + MLA_ROPE)
    w_a = _pad_last(uq, MLA_QK_PAD).reshape(MLA_Q_RANK, H * MLA_QK_PAD)
    w_r = _pad_last(_rotate_half_cols(uq[:, :, MLA_NOPE:]), LANES).reshape(MLA_Q_RANK, H * LANES)
    wuqt = jnp.concatenate([w_a, w_r], axis=1).T.astype(BF16)
    return wd, wkn, wvt, wuqt


def _rope_tables(positions):
    inv_freq = ROPE_THETA ** (-jnp.arange(0, MLA_ROPE, 2, dtype=F32) / MLA_ROPE)
    ang = positions.astype(F32)[:, None] * inv_freq
    cos = jnp.cos(ang)
    sin = jnp.sin(ang)
    cos = _pad_last(jnp.concatenate([cos, cos], axis=-1), LANES)
    sin = _pad_last(jnp.concatenate([sin, sin], axis=-1), LANES)
    return cos, sin, cos.T, sin.T


def _peer_block(x, xb, p, w_q, sub_keys, u, v, ln_g, ln_b, w_gate, w_proj):
    wq = w_q.astype(BF16)
    keys = sub_keys.reshape(2 * PEER_HEADS, PEER_KEYS, PEER_HALF).astype(BF16)
    rk, w2, j1, g1 = _peer_route(x, wq, keys)
    vt = jnp.swapaxes(v.astype(BF16).reshape(-1, EXPERT_TILE, v.shape[1]), 1, 2)
    ffn = _peer_expert(xb, u.astype(BF16), vt, rk, w2, j1, g1)
    return _ffn_ln_ple(x, ffn, p, ln_g[None, :], ln_b[None, :], w_gate.astype(BF16), w_proj.astype(BF16))


def kernel(x, p, positions, ln_g, ln_b, a_w_in, a_b_if, a_hn_g, a_w_out, kv_w_down, kv_norm_g, kv_w_up,
           b_w_dq, b_q_norm_g, b_w_uq, b_w_out, peer_w_q, peer_sub_keys, peer_u, peer_v, ple_w_proj,
           ple_w_gate):
    B, S, D = x.shape
    assert B == 1, "the token axis is treated as one causal sequence"
    xs = x.reshape(B * S, D)
    ps = p.reshape(DEPTH, B * S, PLE_DIM)
    cos, sin, cos_t, sin_t = _rope_tables(positions.reshape(B * S))

    wqv, wog, wg, wkt, wgt, bias_cols, bias_rows = _mlstm_weights(a_w_in[0], a_b_if[0])
    qv, og, g, kt, gt = _mlstm_proj(xs, wqv, wog, wg, wkt, wgt)
    h = _mlstm_scan(qv, kt, g, gt, og, bias_cols, bias_rows, a_hn_g[0][None, :])
    xs, xb = _mix_ln(h, a_w_out[0].astype(BF16), xs, ln_g[0, 0][None, :], ln_b[0, 0][None, :])
    xs, xb = _peer_block(xs, xb, ps[0], peer_w_q[0], peer_sub_keys[0], peer_u[0], peer_v[0],
                         ln_g[0, 1], ln_b[0, 1], ple_w_gate[0], ple_w_proj[0])

    wd, wkn, wvt, wuqt = _mla_weights(kv_w_down, kv_w_up, b_w_uq[0])
    k_all, vt_all = _mla_kv(xb, wd, kv_norm_g[None, :], wkn, wvt, cos, sin)
    qt_all = _mla_q(xb, b_w_dq[0].astype(BF16), b_q_norm_g[0][None, :], wuqt, cos_t, sin_t)
    ot = _attention(qt_all, k_all, vt_all)
    xs, xb = _mix_ln(ot, b_w_out[0].astype(BF16), xs, ln_g[1, 0][None, :], ln_b[1, 0][None, :],
                     a_feature_major=True)
    xs, _ = _peer_block(xs, xb, ps[1], peer_w_q[1], peer_sub_keys[1], peer_u[1], peer_v[1],
                        ln_g[1, 1], ln_b[1, 1], ple_w_gate[1], ple_w_proj[1])
    return xs.reshape(B, S, D)
```

```python
import functools
import math

import jax
import jax.numpy as jnp
from jax import lax
from jax.experimental import pallas as pl
from jax.experimental.pallas import tpu as pltpu

F32 = jnp.float32
BF16 = jnp.bfloat16

LANES = 128
D_MODEL = 1024
DEPTH = 2
DN_ALPHA = (2.0 * DEPTH) ** 0.25
NORM_EPS = 1e-5

ML_HEADS = 8
ML_QK = 64
ML_V = 128
ML_CHUNK = 256

MLA_HEADS = 8
MLA_NOPE = 128
MLA_ROPE = 64
MLA_V = 128
MLA_V_EXT = 144
MLA_KV_RANK = 256
MLA_Q_RANK = 384
MLA_QK_PAD = 256
ROPE_THETA = 10000.0
ATTN_BLOCK = 1024

PEER_HEADS = 8
PEER_KEYS = 128
PEER_HALF = 128
PEER_TOPK = 16
PLE_DIM = 256

ROW_TILE = 512
ROUTE_TILE = 512
ROUTE_GROUP = 256
EXPERT_TOK_TILE = 1024
EXPERT_TILE = 1024
VMEM_LIMIT = 56 * 1024 * 1024

NEG_INF = float("-inf")


def _params(*sem):
    return pltpu.CompilerParams(dimension_semantics=sem, vmem_limit_bytes=VMEM_LIMIT)


def _dot(a, b):
    return jnp.dot(a, b, preferred_element_type=F32)


def _dot_nt(a, b):
    return lax.dot_general(a, b, (((1,), (1,)), ((), ())), preferred_element_type=F32)


def _layer_norm(y, g, b):
    mu = jnp.mean(y, axis=-1, keepdims=True)
    yc = y - mu
    var = jnp.mean(yc * yc, axis=-1, keepdims=True)
    return yc * lax.rsqrt(var + NORM_EPS) * g + b


def _rms_norm(y, g):
    return y * lax.rsqrt(jnp.mean(y * y, axis=-1, keepdims=True) + NORM_EPS) * g


def _full(shape):
    nd = len(shape)
    return pl.BlockSpec(shape, lambda *_: (0,) * nd)


def _mlstm_proj_kernel(x_ref, wqv_ref, wog_ref, wg_ref, wkt_ref, wgt_ref,
                       qv_ref, og_ref, g_ref, kt_ref, gt_ref):
    xb = x_ref[...].astype(BF16)
    qv_ref[...] = _dot(xb, wqv_ref[...]).astype(BF16)
    og_ref[...] = _dot(xb, wog_ref[...])
    g_ref[...] = _dot(xb, wg_ref[...])
    kt_ref[...] = _dot_nt(wkt_ref[...], xb).astype(BF16)
    gt_ref[...] = _dot_nt(wgt_ref[...], xb)


def _mlstm_proj(x, wqv, wog, wg, wkt, wgt):
    T, D = x.shape
    tm = ROW_TILE
    nqv, nk, ng = wqv.shape[1], wkt.shape[0], wgt.shape[0]
    return pl.pallas_call(
        _mlstm_proj_kernel,
        grid=(T // tm,),
        in_specs=[pl.BlockSpec((tm, D), lambda i: (i, 0)),
                  _full(wqv.shape), _full(wog.shape), _full(wg.shape), _full(wkt.shape), _full(wgt.shape)],
        out_specs=[pl.BlockSpec((tm, nqv), lambda i: (i, 0)),
                   pl.BlockSpec((tm, D), lambda i: (i, 0)),
                   pl.BlockSpec((tm, LANES), lambda i: (i, 0)),
                   pl.BlockSpec((nk, tm), lambda i: (0, i)),
                   pl.BlockSpec((ng, tm), lambda i: (0, i))],
        out_shape=[jax.ShapeDtypeStruct((T, nqv), BF16),
                   jax.ShapeDtypeStruct((T, D), F32),
                   jax.ShapeDtypeStruct((T, LANES), F32),
                   jax.ShapeDtypeStruct((nk, T), BF16),
                   jax.ShapeDtypeStruct((ng, T), F32)],
        compiler_params=_params("parallel"),
    )(x, wqv, wog, wg, wkt, wgt)


def _split3(x):
    hi = x.astype(BF16)
    r1 = x - hi.astype(F32)
    mid = r1.astype(BF16)
    lo = (r1 - mid.astype(F32)).astype(BF16)
    return hi, mid, lo


def _log_sigmoid(x):
    return jnp.minimum(x, 0.0) - jnp.log1p(jnp.exp(-jnp.abs(x)))


def _mlstm_scan_kernel(q_ref, v_ref, kt_ref, g_ref, gt_ref, og_ref, bc_ref, br_ref, hng_ref,
                       o_ref, c_ref, m_ref):
    H = ML_HEADS
    L = q_ref.shape[0]

    @pl.when(pl.program_id(0) == 0)
    def _():
        c_ref[...] = jnp.zeros_like(c_ref)
        m_ref[...] = jnp.zeros_like(m_ref)

    row = lax.broadcasted_iota(jnp.int32, (L, L), 0)
    col = lax.broadcasted_iota(jnp.int32, (L, L), 1)
    causal = col <= row
    tril = jnp.where(causal, 1.0, 0.0).astype(BF16)
    triu = jnp.where(row <= col, 1.0, 0.0).astype(BF16)

    g = g_ref[...] + bc_ref[...]
    gt = gt_ref[...] + br_ref[...]
    b_cols = sum(_dot(tril, piece) for piece in _split3(_log_sigmoid(g)))
    b_rows = sum(_dot(piece, triu) for piece in _split3(_log_sigmoid(gt)))

    lane = lax.broadcasted_iota(jnp.int32, (L, LANES), 1)
    ones_col = jnp.where(lane == 0, 1.0, 0.0).astype(BF16)

    for h in range(H):
        b_c = b_cols[:, H + h:H + h + 1]
        b_r = b_rows[H + h:H + h + 1, :]
        ig_r = gt[h:h + 1, :]
        m_prev = m_ref[h:h + 1, 0:1]
        dmat = jnp.where(causal, b_c - b_r + ig_r, NEG_INF)
        log_inter = b_c + m_prev
        m_t = jnp.maximum(log_inter, jnp.max(dmat, axis=1, keepdims=True))
        w = jnp.exp(dmat - m_t)
        s_inter = jnp.exp(log_inter - m_t)

        q = q_ref[:, h * LANES:(h + 1) * LANES]
        kt = kt_ref[h * LANES:(h + 1) * LANES, :]
        v_ext = jnp.concatenate([v_ref[:, h * ML_V:(h + 1) * ML_V], ones_col], axis=1)
        s = (_dot(q, kt) * w).astype(BF16)
        c_prev = c_ref[h]
        nd = _dot(s, v_ext) + s_inter * _dot(q, c_prev.astype(BF16))
        num = nd[:, :ML_V]
        den = nd[:, ML_V:ML_V + 1]
        hh = num / jnp.maximum(jnp.abs(den), jnp.exp(-m_t))

        b_last = b_r[:, L - 1:L]
        log_src = b_last - b_r + ig_r
        m_new = jnp.maximum(b_last + m_prev, jnp.max(log_src, axis=1, keepdims=True))
        w_src = jnp.exp(log_src - m_new)
        s_old = jnp.exp(b_last + m_prev - m_new)
        c_ref[h] = s_old * c_prev + _dot((kt.astype(F32) * w_src).astype(BF16), v_ext)
        m_ref[h:h + 1, :] = jnp.broadcast_to(m_new, (1, LANES))

        mu = jnp.mean(hh, axis=-1, keepdims=True)
        hc = hh - mu
        var = jnp.mean(hc * hc, axis=-1, keepdims=True)
        hn = hc * lax.rsqrt(var + NORM_EPS) * hng_ref[:, h * ML_V:(h + 1) * ML_V]
        og = og_ref[:, h * ML_V:(h + 1) * ML_V]
        o_ref[:, h * ML_V:(h + 1) * ML_V] = (jax.nn.sigmoid(og) * hn).astype(o_ref.dtype)


def _mlstm_scan(qv, kt, g, gt, og, bias_cols, bias_rows, hn_g):
    T = qv.shape[0]
    L = ML_CHUNK
    D = D_MODEL
    return pl.pallas_call(
        _mlstm_scan_kernel,
        grid=(T // L,),
        in_specs=[pl.BlockSpec((L, D), lambda c: (c, 0)),
                  pl.BlockSpec((L, D), lambda c: (c, 1)),
                  pl.BlockSpec((D, L), lambda c: (0, c)),
                  pl.BlockSpec((L, LANES), lambda c: (c, 0)),
                  pl.BlockSpec((2 * ML_HEADS, L), lambda c: (0, c)),
                  pl.BlockSpec((L, D), lambda c: (c, 0)),
                  _full(bias_cols.shape), _full(bias_rows.shape), _full(hn_g.shape)],
        out_specs=pl.BlockSpec((L, D), lambda c: (c, 0)),
        out_shape=jax.ShapeDtypeStruct((T, D), BF16),
        scratch_shapes=[pltpu.VMEM((ML_HEADS, LANES, 2 * LANES), F32),
                        pltpu.VMEM((ML_HEADS, LANES), F32)],
        compiler_params=_params("arbitrary"),
    )(qv, qv, kt, g, gt, og, bias_cols, bias_rows, hn_g)


def _mix_ln_kernel(a_ref, w_ref, x_ref, g_ref, b_ref, o_ref, ob_ref, *, a_feature_major):
    if a_feature_major:
        mix = lax.dot_general(a_ref[...], w_ref[...], (((0,), (0,)), ((), ())), preferred_element_type=F32)
    else:
        mix = _dot(a_ref[...], w_ref[...])
    out = _layer_norm(mix + DN_ALPHA * x_ref[...], g_ref[...], b_ref[...])
    o_ref[...] = out
    ob_ref[...] = out.astype(BF16)


def _mix_ln(a, w, x, g, b, *, a_feature_major=False):
    T, D = x.shape
    tm = ROW_TILE
    if a_feature_major:
        a_spec = pl.BlockSpec((a.shape[0], tm), lambda i: (0, i))
    else:
        a_spec = pl.BlockSpec((tm, a.shape[1]), lambda i: (i, 0))
    return pl.pallas_call(
        functools.partial(_mix_ln_kernel, a_feature_major=a_feature_major),
        grid=(T // tm,),
        in_specs=[a_spec, _full(w.shape),
                  pl.BlockSpec((tm, D), lambda i: (i, 0)), _full(g.shape), _full(b.shape)],
        out_specs=[pl.BlockSpec((tm, D), lambda i: (i, 0))] * 2,
        out_shape=[jax.ShapeDtypeStruct((T, D), F32), jax.ShapeDtypeStruct((T, D), BF16)],
        compiler_params=_params("parallel"),
    )(a, w, x, g, b)


PEER_NTOP = PEER_TOPK + 1
PEER_TOP_ROWS = 24


def _peer_candidate_blocks(a_ref, b_ref):
    blocks = [a_ref[...] + b_ref[0:1, :], b_ref[...] + a_ref[0:1, :]]
    for j in range(1, 8):
        blocks.append(a_ref[0:8, :] + b_ref[j:j + 1, :])
    return jnp.concatenate(blocks, axis=0)


def _bf16_pair_words(x):
    bits = lax.bitcast_convert_type(x, jnp.uint32)
    r = (bits + jnp.uint32(0x7FFF) + ((bits >> 16) & jnp.uint32(1))) >> 16
    return r | (r << 16)


def _peer_route_kernel(x_ref, wq_ref, keys_ref, rk_ref, w2_ref, j1_ref, g1_ref, s_ref, top_ref):
    q = _dot(x_ref[...].astype(BF16), wq_ref[...]).astype(BF16)
    for hc in range(2 * PEER_HEADS):
        s_ref[hc] = _dot_nt(keys_ref[hc], q[:, hc * PEER_HALF:(hc + 1) * PEER_HALF])
    top_ref[...] = jnp.full(top_ref.shape, NEG_INF, F32)
    n_groups = s_ref.shape[2] // ROUTE_GROUP

    def head_group(idx, carry):
        h = idx // n_groups
        ls = pl.ds(pl.multiple_of((idx % n_groups) * ROUTE_GROUP, ROUTE_GROUP), ROUTE_GROUP)
        halves = []
        for c in range(2):
            s = s_ref[2 * h + c, :, ls]
            cur = s
            rank = jnp.full(s.shape, float(PEER_KEYS - 1), F32)
            for r in range(PEER_NTOP):
                mx = jnp.max(cur, axis=0, keepdims=True)
                top_ref[c, r:r + 1, :] = mx
                hit = cur == mx
                if c == 1:
                    rank = jnp.where(hit, float(r), rank)
                cur = jnp.where(hit, NEG_INF, cur)
            halves.append((s, rank))
        (s1, _), (s2, rank2) = halves
        cand = _peer_candidate_blocks(top_ref.at[0], top_ref.at[1])
        a0 = top_ref[0, 0:1, :]
        b0 = top_ref[1, 0:1, :]
        best = a0 + b0
        z = jnp.zeros_like(best)
        t_prev = best
        t_cur = best
        for r in range(PEER_NTOP):
            t_prev = t_cur
            t_cur = jnp.max(cand, axis=0, keepdims=True)
            if r < PEER_TOPK:
                z = z + jnp.exp(t_cur - best)
            cand = jnp.where(cand == t_cur, NEG_INF, cand)
        tau = 0.5 * (t_prev + t_cur)
        thr = tau - s1
        count = jnp.zeros_like(s1)
        for r in range(PEER_NTOP):
            count = jnp.where(top_ref[1, r:r + 1, :] >= thr, float(r + 1), count)
        rk_ref[h, :, ls] = pltpu.bitcast(rank2.astype(BF16), jnp.uint32)
        w2_ref[h, :, ls] = pltpu.bitcast(jnp.exp(s2 - b0).astype(BF16), jnp.uint32)
        j1_ref[h, :, ls] = _bf16_pair_words(count)
        g1_ref[h, :, ls] = _bf16_pair_words(jnp.exp(s1 - a0) / z)
        return carry

    lax.fori_loop(0, PEER_HEADS * n_groups, head_group, 0)


def _peer_route(x, wq, keys):
    T, D = x.shape
    tt = ROUTE_TILE
    ospec = pl.BlockSpec((PEER_HEADS, PEER_KEYS, tt), lambda i: (0, 0, i))
    return pl.pallas_call(
        _peer_route_kernel,
        grid=(T // tt,),
        in_specs=[pl.BlockSpec((tt, D), lambda i: (i, 0)), _full(wq.shape), _full(keys.shape)],
        out_specs=[pl.BlockSpec((PEER_HEADS, PEER_KEYS // 2, tt), lambda i: (0, 0, i))] * 2 + [ospec] * 2,
        out_shape=[jax.ShapeDtypeStruct((PEER_HEADS, PEER_KEYS // 2, T), jnp.uint32)] * 2
        + [jax.ShapeDtypeStruct((PEER_HEADS, PEER_KEYS, T), jnp.uint32)] * 2,
        scratch_shapes=[pltpu.VMEM((2 * PEER_HEADS, PEER_KEYS, tt), F32), pltpu.VMEM((2, PEER_TOP_ROWS, ROUTE_GROUP), F32)],
        compiler_params=_params("parallel"),
    )(x, wq, keys)


def _gelu(x):
    return 0.5 * x * (1.0 + lax.erf(x * (1.0 / math.sqrt(2.0))))


GATE_ROWS = 64
GATE_KEYS = 8


def _row_bf16(words, rows):
    return pltpu.bitcast(jnp.broadcast_to(words, (rows // 2, words.shape[1])), BF16)


def _peer_expert_kernel(x_ref, u_ref, vt_ref, rk_ref, w2_ref, j1_ref, g1_ref, o_ref,
                        acc_ref, act0, act1, p0, p1, *, n_tiles):
    s = pl.program_id(1)
    last = pl.num_programs(1) - 1
    tt = x_ref.shape[0]
    n_sub = u_ref.shape[0] // PEER_KEYS
    n_rb = PEER_KEYS // GATE_ROWS

    @pl.when(s == 0)
    def _():
        acc_ref[...] = jnp.zeros_like(acc_ref)
        act0[...] = _dot_nt(u_ref[...].astype(BF16), x_ref[...])

    def gate_block(act_ref, p_ref, lc, rb, kb):
        ls = slice(lc * LANES, (lc + 1) * LANES)
        ws = slice(rb * (GATE_ROWS // 2), (rb + 1) * (GATE_ROWS // 2))
        keys = range(kb * GATE_KEYS, (kb + 1) * GATE_KEYS)
        gates = {}
        for h in range(PEER_HEADS):
            rk = pltpu.bitcast(rk_ref[h, ws, ls], BF16)
            w = pltpu.bitcast(w2_ref[h, ws, ls], BF16)
            for a in keys:
                count = _row_bf16(j1_ref[h, a:a + 1, ls], GATE_ROWS)
                g = _row_bf16(g1_ref[h, a:a + 1, ls], GATE_ROWS)
                term = jnp.where(rk < count, w, jnp.zeros_like(w)) * g
                gates[a] = term if h == 0 else gates[a] + term
        for a in keys:
            es = slice(a * PEER_KEYS + rb * GATE_ROWS, a * PEER_KEYS + (rb + 1) * GATE_ROWS)
            p_ref[es, ls] = gates[a] * _gelu(act_ref[es, ls].astype(BF16))

    def stages(parity, first=True, second=True):
        act_w, act_r, p_w, p_r = (act0, act1, p1, p0) if parity == 0 else (act1, act0, p0, p1)
        if second:
            acc_ref[...] += _dot(vt_ref[...], p_r[...])
        if first:
            act_w[...] = _dot_nt(u_ref[...].astype(BF16), x_ref[...])
        for lc in range(tt // LANES):
            for rb in range(n_rb):
                for kb in range(n_sub // GATE_KEYS):
                    gate_block(act_r, p_w, lc, rb, kb)

    assert n_tiles % 2 == 0 and n_tiles >= 4
    steady = jnp.logical_and(s > 1, s < n_tiles)

    @pl.when(s == 1)
    def _():
        stages(1, second=False)

    @pl.when(jnp.logical_and(steady, s % 2 == 0))
    def _():
        stages(0)

    @pl.when(jnp.logical_and(steady, s % 2 == 1))
    def _():
        stages(1)

    @pl.when(s == n_tiles)
    def _():
        stages(0, first=False)

    @pl.when(s == last)
    def _():
        p_last = p1 if n_tiles % 2 == 0 else p0
        o_ref[...] = (acc_ref[...] + _dot(vt_ref[...], p_last[...])).T


def _peer_expert(xb, u, vt, rk, w2, j1, g1):
    T, D = xb.shape
    E = u.shape[0]
    tt, et = EXPERT_TOK_TILE, EXPERT_TILE
    n_e = E // et
    rspec = pl.BlockSpec((PEER_HEADS, PEER_KEYS // 2, tt), lambda i, s: (0, 0, i))
    sspec = pl.BlockSpec((PEER_HEADS, et // PEER_KEYS, tt),
                         lambda i, s: (0, jnp.clip(s - 1, 0, n_e - 1), i))
    return pl.pallas_call(
        functools.partial(_peer_expert_kernel, n_tiles=n_e),
        grid=(T // tt, n_e + 2),
        in_specs=[pl.BlockSpec((tt, D), lambda i, s: (i, 0)),
                  pl.BlockSpec((et, D), lambda i, s: (jnp.minimum(s, n_e - 1), 0)),
                  pl.BlockSpec((None, D, et), lambda i, s: (jnp.clip(s - 2, 0, n_e - 1), 0, 0)),
                  rspec, rspec, sspec, sspec],
        out_specs=pl.BlockSpec((tt, D), lambda i, s: (i, 0)),
        out_shape=jax.ShapeDtypeStruct((T, D), F32),
        scratch_shapes=[pltpu.VMEM((D, tt), F32), pltpu.VMEM((et, tt), F32), pltpu.VMEM((et, tt), F32),
                        pltpu.VMEM((et, tt), BF16), pltpu.VMEM((et, tt), BF16)],
        compiler_params=_params("parallel", "arbitrary"),
    )(xb, u, vt, rk, w2, j1, g1)


def _ffn_ln_ple_kernel(x_ref, f_ref, p_ref, g_ref, b_ref, wg_ref, wp_ref, o_ref, ob_ref):
    x2 = _layer_norm(DN_ALPHA * x_ref[...] + f_ref[...], g_ref[...], b_ref[...])
    gate = jax.nn.sigmoid(_dot(x2.astype(BF16), wg_ref[...]))
    pe = _dot(p_ref[...].astype(BF16), wp_ref[...])
    out = x2 + gate * pe
    o_ref[...] = out
    ob_ref[...] = out.astype(BF16)


def _ffn_ln_ple(x, f, p, g, b, wg, wp):
    T, D = x.shape
    tm = ROW_TILE
    return pl.pallas_call(
        _ffn_ln_ple_kernel,
        grid=(T // tm,),
        in_specs=[pl.BlockSpec((tm, D), lambda i: (i, 0)), pl.BlockSpec((tm, D), lambda i: (i, 0)),
                  pl.BlockSpec((tm, p.shape[1]), lambda i: (i, 0)),
                  _full(g.shape), _full(b.shape), _full(wg.shape), _full(wp.shape)],
        out_specs=[pl.BlockSpec((tm, D), lambda i: (i, 0))] * 2,
        out_shape=[jax.ShapeDtypeStruct((T, D), F32), jax.ShapeDtypeStruct((T, D), BF16)],
        compiler_params=_params("parallel"),
    )(x, f, p, g, b, wg, wp)


def _mla_kv_kernel(x_ref, wd_ref, g_ref, wkn_ref, wvt_ref, cos_ref, sin_ref, k_ref, vt_ref):
    ckr = _dot(x_ref[...], wd_ref[...])
    ckv = _rms_norm(ckr[:, :MLA_KV_RANK], g_ref[...]).astype(BF16)
    kn = _dot(ckv, wkn_ref[...])
    vt = _dot_nt(wvt_ref[...], ckv).astype(BF16)
    pad_row = lax.broadcasted_iota(jnp.int32, (MLA_V_EXT - MLA_V, vt.shape[1]), 0)
    ones_rows = jnp.where(pad_row == 0, 1.0, 0.0).astype(BF16)
    for h in range(MLA_HEADS):
        vt_ref[h * MLA_V_EXT:h * MLA_V_EXT + MLA_V, :] = vt[h * MLA_V:(h + 1) * MLA_V, :]
        vt_ref[h * MLA_V_EXT + MLA_V:(h + 1) * MLA_V_EXT, :] = ones_rows
    kr = (ckr[:, MLA_KV_RANK:MLA_KV_RANK + LANES] * cos_ref[...]
          + ckr[:, MLA_KV_RANK + LANES:MLA_KV_RANK + 2 * LANES] * sin_ref[...]).astype(BF16)
    for h in range(MLA_HEADS):
        k_ref[:, h * MLA_QK_PAD:h * MLA_QK_PAD + MLA_NOPE] = kn[:, h * MLA_NOPE:(h + 1) * MLA_NOPE].astype(BF16)
        k_ref[:, h * MLA_QK_PAD + MLA_NOPE:(h + 1) * MLA_QK_PAD] = kr


def _mla_kv(xb, wd, g, wkn, wvt, cos, sin):
    T, D = xb.shape
    tm = ROW_TILE
    return pl.pallas_call(
        _mla_kv_kernel,
        grid=(T // tm,),
        in_specs=[pl.BlockSpec((tm, D), lambda i: (i, 0)), _full(wd.shape), _full(g.shape),
                  _full(wkn.shape), _full(wvt.shape),
                  pl.BlockSpec((tm, LANES), lambda i: (i, 0)), pl.BlockSpec((tm, LANES), lambda i: (i, 0))],
        out_specs=[pl.BlockSpec((tm, MLA_HEADS * MLA_QK_PAD), lambda i: (i, 0)),
                   pl.BlockSpec((MLA_HEADS * MLA_V_EXT, tm), lambda i: (0, i))],
        out_shape=[jax.ShapeDtypeStruct((T, MLA_HEADS * MLA_QK_PAD), BF16),
                   jax.ShapeDtypeStruct((MLA_HEADS * MLA_V_EXT, T), BF16)],
        compiler_params=_params("parallel"),
    )(xb, wd, g, wkn, wvt, cos, sin)


def _mla_q_kernel(x_ref, wdq_ref, g_ref, wuqt_ref, cost_ref, sint_ref, qt_ref):
    scale = (MLA_NOPE + MLA_ROPE) ** -0.5 * math.log2(math.e)
    cq = _rms_norm(_dot(x_ref[...], wdq_ref[...]), g_ref[...]).astype(BF16)
    qa = _dot_nt(wuqt_ref[...], cq)
    rot0 = MLA_HEADS * MLA_QK_PAD
    for h in range(MLA_HEADS):
        base = h * MLA_QK_PAD
        qt_ref[base:base + MLA_NOPE, :] = (qa[base:base + MLA_NOPE, :] * scale).astype(BF16)
        rope = (qa[base + MLA_NOPE:base + MLA_QK_PAD, :] * cost_ref[...]
                + qa[rot0 + h * LANES:rot0 + (h + 1) * LANES, :] * sint_ref[...])
        qt_ref[base + MLA_NOPE:base + MLA_QK_PAD, :] = (rope * scale).astype(BF16)


def _mla_q(xb, wdq, g, wuqt, cost, sint):
    T, D = xb.shape
    tm = ROW_TILE
    return pl.pallas_call(
        _mla_q_kernel,
        grid=(T // tm,),
        in_specs=[pl.BlockSpec((tm, D), lambda i: (i, 0)), _full(wdq.shape), _full(g.shape), _full(wuqt.shape),
                  pl.BlockSpec((LANES, tm), lambda i: (0, i)), pl.BlockSpec((LANES, tm), lambda i: (0, i))],
        out_specs=pl.BlockSpec((MLA_HEADS * MLA_QK_PAD, tm), lambda i: (0, i)),
        out_shape=jax.ShapeDtypeStruct((MLA_HEADS * MLA_QK_PAD, T), BF16),
        compiler_params=_params("parallel"),
    )(xb, wdq, g, wuqt, cost, sint)


def _attn_kernel(qt_ref, k_ref, vt_ref, o_ref, m_ref, acc_ref, s_a, s_b):
    i = pl.program_id(1)
    tq = qt_ref.shape[1]
    m_ref[...] = jnp.full(m_ref.shape, NEG_INF, F32)
    acc_ref[...] = jnp.zeros_like(acc_ref)

    def scores(j, s_ref):
        k = k_ref[pl.ds(pl.multiple_of(j * tq, tq), tq), :]
        s_ref[...] = _dot(k, qt_ref[...])

    def absorb(j, s_ref, masked):
        vt = vt_ref[:, pl.ds(pl.multiple_of(j * tq, tq), tq)]
        s = s_ref[...]
        if masked:
            key = lax.broadcasted_iota(jnp.int32, s.shape, 0)
            qry = lax.broadcasted_iota(jnp.int32, s.shape, 1)
            s = jnp.where(key <= qry, s, NEG_INF)
        m_prev = m_ref[...]
        m_new = jnp.maximum(m_prev, jnp.max(s, axis=0, keepdims=True))
        p = jnp.exp2(s - m_new)
        alpha = jnp.exp2(m_prev - m_new)
        acc_ref[...] = alpha * acc_ref[...] + _dot(vt, p.astype(BF16))
        m_ref[...] = m_new

    scores(0, s_a)

    def pair(jj, carry):
        scores(2 * jj + 1, s_b)
        absorb(2 * jj, s_a, False)
        scores(2 * jj + 2, s_a)
        absorb(2 * jj + 1, s_b, False)
        return carry

    lax.fori_loop(0, i // 2, pair, 0)

    @pl.when(i % 2 == 0)
    def _():
        absorb(i, s_a, True)

    @pl.when(i % 2 == 1)
    def _():
        scores(i, s_b)
        absorb(i - 1, s_a, False)
        absorb(i, s_b, True)

    o_ref[...] = (acc_ref[:MLA_V, :] / acc_ref[MLA_V:MLA_V + 1, :]).astype(o_ref.dtype)


def _attention(qt, k, vt):
    T = k.shape[0]
    tq = ATTN_BLOCK
    return pl.pallas_call(
        _attn_kernel,
        grid=(MLA_HEADS, T // tq),
        in_specs=[pl.BlockSpec((MLA_QK_PAD, tq), lambda h, i: (h, i)),
                  pl.BlockSpec((T, MLA_QK_PAD), lambda h, i: (0, h)),
                  pl.BlockSpec((MLA_V_EXT, T), lambda h, i: (h, 0))],
        out_specs=pl.BlockSpec((MLA_V, tq), lambda h, i: (h, i)),
        out_shape=jax.ShapeDtypeStruct((MLA_HEADS * MLA_V, T), BF16),
        scratch_shapes=[pltpu.VMEM((1, tq), F32), pltpu.VMEM((MLA_V_EXT, tq), F32),
                        pltpu.VMEM((tq, tq), F32), pltpu.VMEM((tq, tq), F32)],
        compiler_params=_params("parallel", "arbitrary"),
    )(qt, k, vt)


def _pad_last(w, width):
    return jnp.pad(w, [(0, 0)] * (w.ndim - 1) + [(0, width - w.shape[-1])])


def _rotate_half_cols(w):
    half = w.shape[-1] // 2
    return jnp.concatenate([-w[..., half:], w[..., :half]], axis=-1)


def _mlstm_weights(w_in, b_if):
    D = D_MODEL
    H = ML_HEADS
    q_end = H * ML_QK
    k_end = 2 * q_end
    v_end = k_end + H * ML_V
    i_end = v_end + H
    f_end = i_end + H
    wq = _pad_last(w_in[:, :q_end].reshape(D, H, ML_QK), LANES).reshape(D, H * LANES)
    wk = _pad_last((w_in[:, q_end:k_end] * (ML_QK ** -0.5)).reshape(D, H, ML_QK), LANES).reshape(D, H * LANES)
    wqv = jnp.concatenate([wq, w_in[:, k_end:v_end]], axis=1).astype(BF16)
    wog = w_in[:, f_end:].astype(BF16)
    wgate = w_in[:, v_end:f_end]
    wg = _pad_last(wgate, LANES).astype(BF16)
    wkt = wk.T.astype(BF16)
    wgt = wgate.T.astype(BF16)
    bias = jnp.concatenate([b_if[0], b_if[1]]).astype(F32)
    bias_cols = _pad_last(bias[None, :], LANES)
    bias_rows = jnp.broadcast_to(bias[:, None], (2 * H, ML_CHUNK))
    return wqv, wog, wg, wkt, wgt, bias_cols, bias_rows


def _mla_weights(kv_w_down, kv_w_up, w_uq):
    H = MLA_HEADS
    wr = kv_w_down[:, MLA_KV_RANK:]
    wd = jnp.concatenate([kv_w_down[:, :MLA_KV_RANK], _pad_last(wr, LANES),
                          _pad_last(_rotate_half_cols(wr), LANES)], axis=1).astype(BF16)
    up = kv_w_up.reshape(MLA_KV_RANK, H, MLA_NOPE + MLA_V)
    wkn = up[:, :, :MLA_NOPE].reshape(MLA_KV_RANK, H * MLA_NOPE).astype(BF16)
    wvt = up[:, :, MLA_NOPE:].reshape(MLA_KV_RANK, H * MLA_V).T.astype(BF16)
    uq = w_uq.reshape(MLA_Q_RANK, H, MLA_NOPE + MLA_ROPE)
    w_a = _pad_last(uq, MLA_QK_PAD).reshape(MLA_Q_RANK, H * MLA_QK_PAD)
    w_r = _pad_last(_rotate_half_cols(uq[:, :, MLA_NOPE:]), LANES).reshape(MLA_Q_RANK, H * LANES)
    wuqt = jnp.concatenate([w_a, w_r], axis=1).T.astype(BF16)
    return wd, wkn, wvt, wuqt


def _rope_tables(positions):
    inv_freq = ROPE_THETA ** (-jnp.arange(0, MLA_ROPE, 2, dtype=F32) / MLA_ROPE)
    ang = positions.astype(F32)[:, None] * inv_freq
    cos = jnp.cos(ang)
    sin = jnp.sin(ang)
    cos = _pad_last(jnp.concatenate([cos, cos], axis=-1), LANES)
    sin = _pad_last(jnp.concatenate([sin, sin], axis=-1), LANES)
    return cos, sin, cos.T, sin.T


def _peer_block(x, xb, p, w_q, sub_keys, u, v, ln_g, ln_b, w_gate, w_proj):
    wq = w_q.astype(BF16)
    keys = sub_keys.reshape(2 * PEER_HEADS, PEER_KEYS, PEER_HALF).astype(BF16)
    rk, w2, j1, g1 = _peer_route(x, wq, keys)
    vt = jnp.swapaxes(v.astype(BF16).reshape(-1, EXPERT_TILE, v.shape[1]), 1, 2)
    ffn = _peer_expert(xb, u, vt, rk, w2, j1, g1)
    return _ffn_ln_ple(x, ffn, p, ln_g[None, :], ln_b[None, :], w_gate.astype(BF16), w_proj.astype(BF16))


def kernel(x, p, positions, ln_g, ln_b, a_w_in, a_b_if, a_hn_g, a_w_out, kv_w_down, kv_norm_g, kv_w_up,
           b_w_dq, b_q_norm_g, b_w_uq, b_w_out, peer_w_q, peer_sub_keys, peer_u, peer_v, ple_w_proj,
           ple_w_gate):
    B, S, D = x.shape
    assert B == 1, "the token axis is treated as one causal sequence"
    xs = x.reshape(B * S, D)
    ps = p.reshape(DEPTH, B * S, PLE_DIM)
    cos, sin, cos_t, sin_t = _rope_tables(positions.reshape(B * S))

    wqv, wog, wg, wkt, wgt, bias_cols, bias_rows = _mlstm_weights(a_w_in[0], a_b_if[0])
    qv, og, g, kt, gt = _mlstm_proj(xs, wqv, wog, wg, wkt, wgt)
    h = _mlstm_scan(qv, kt, g, gt, og, bias_cols, bias_rows, a_hn_g[0][None, :])
    xs, xb = _mix_ln(h, a_w_out[0].astype(BF16), xs, ln_g[0, 0][None, :], ln_b[0, 0][None, :])
    xs, xb = _peer_block(xs, xb, ps[0], peer_w_q[0], peer_sub_keys[0], peer_u[0], peer_v[0],
                         ln_g[0, 1], ln_b[0, 1], ple_w_gate[0], ple_w_proj[0])

    wd, wkn, wvt, wuqt = _mla_weights(kv_w_down, kv_w_up, b_w_uq[0])
    k_all, vt_all = _mla_kv(xb, wd, kv_norm_g[None, :], wkn, wvt, cos, sin)
    qt_all = _mla_q(xb, b_w_dq[0].astype(BF16), b_q_norm_g[0][None, :], wuqt, cos_t, sin_t)
    ot = _attention(qt_all, k_all, vt_all)
    xs, xb = _mix_ln(ot, b_w_out[0].astype(BF16), xs, ln_g[1, 0][None, :], ln_b[1, 0][None, :],
                     a_feature_major=True)
    xs, _ = _peer_block(xs, xb, ps[1], peer_w_q[1], peer_sub_keys[1], peer_u[1], peer_v[1],
                        ln_g[1, 1], ln_b[1, 1], ple_w_gate[1], ple_w_proj[1])
    return xs.reshape(B, S, D)
```
